```python
import math
import jax, jax.numpy as jnp
from jax import lax
import numpy as np

D_MODEL = 2048
BATCH = 2
SEQ = 16384
DEPTH = 1
DEC_BATCH = 16
DEC_SEQ = 64
PAST_LEN = 2048

CHUNK = 64
Q_BLOCK = 128
EPS = 1e-6
FORGET_BIAS_INIT = 2.0
FOX_HEAD_DIM = 64
FOX_DIM = D_MODEL // 2
FOX_HEADS = FOX_DIM // FOX_HEAD_DIM
SSD_HEAD_DIM = 64
SSD_DIM = D_MODEL - FOX_DIM
SSD_HEADS = SSD_DIM // SSD_HEAD_DIM
SSD_GROUPS = 2
SSD_HPG = SSD_HEADS // SSD_GROUPS
SSD_STATE = 128
CONV_W = 4
CONV_DIM = SSD_DIM + 2 * SSD_GROUPS * SSD_STATE
MIX_DIM = FOX_DIM + SSD_DIM
IN_DIM = 3 * FOX_DIM + FOX_HEADS + SSD_DIM + CONV_DIM + SSD_HEADS
D_FF = ((8 * D_MODEL + 3 * 256 - 1) // (3 * 256)) * 256

kernel_name = "fox_ssd_parallel_hybrid_stream_step"


def rmsnorm(x, g):
    xf = x.astype(jnp.float32)
    y = xf * lax.rsqrt(jnp.mean(xf * xf, axis=-1, keepdims=True) + EPS)
    return (y * g).astype(x.dtype)


def gated_rmsnorm(y, z, g):
    b, T, _ = y.shape
    u = (y.astype(jnp.float32) * jax.nn.silu(z.astype(jnp.float32)))
    u = u.reshape(b, T, SSD_GROUPS, SSD_DIM // SSD_GROUPS)
    u = u * lax.rsqrt(jnp.mean(u * u, axis=-1, keepdims=True) + EPS)
    return u.reshape(b, T, SSD_DIM) * g


def causal_dwconv(xpad, w, bias):
    T = xpad.shape[1] - (CONV_W - 1)
    out = bias
    for j in range(CONV_W):
        out = out + xpad[:, j:j + T] * w[j]
    return out


def fox_block(q, k, v, fq, fk, qpos, kpos):
    s = jnp.einsum("bthd,bshd->bhts", q, k).astype(jnp.float32) * (FOX_HEAD_DIM ** -0.5)
    bias = jnp.swapaxes(fq, 1, 2)[:, :, :, None] - jnp.swapaxes(fk, 1, 2)[:, :, None, :]
    causal = kpos[None, :] <= qpos[:, None]
    s = jnp.where(causal, s + bias, -jnp.inf)
    p = jax.nn.softmax(s, axis=-1)
    return jnp.einsum("bhts,bshd->bthd", p.astype(v.dtype), v)


def fox_attention(q, k, v, fq, fk, qpos, kpos):
    b, T = q.shape[0], q.shape[1]
    if T <= Q_BLOCK:
        return fox_block(q, k, v, fq, fk, qpos, kpos)
    nb = T // Q_BLOCK

    def blocks(a):
        return jnp.moveaxis(a.reshape((b, nb, Q_BLOCK) + a.shape[2:]), 1, 0)

    out = lax.map(lambda xs: fox_block(xs[0], k, v, xs[1], fk, xs[2], kpos),
                  (blocks(q), blocks(fq), qpos.reshape(nb, Q_BLOCK)))
    return jnp.moveaxis(out, 0, 1).reshape(b, T, FOX_HEADS, FOX_HEAD_DIM)


def ssd_scan(xs, dt, A, Bm, Cm, h0):
    b, T = xs.shape[0], xs.shape[1]
    l = T if T <= CHUNK else CHUNK
    nc = T // l
    x = xs.reshape(b, nc, l, SSD_GROUPS, SSD_HPG, SSD_HEAD_DIM)
    dtc = dt.reshape(b, nc, l, SSD_GROUPS, SSD_HPG)
    Bc = Bm.reshape(b, nc, l, SSD_GROUPS, SSD_STATE)
    Cc = Cm.reshape(b, nc, l, SSD_GROUPS, SSD_STATE)
    cum = jnp.cumsum(dtc * A.reshape(SSD_GROUPS, SSD_HPG), axis=2)
    causal = jnp.tril(jnp.ones((l, l), dtype=bool))[:, :, None, None]
    seg = cum[:, :, :, None] - cum[:, :, None, :]
    decay = jnp.exp(jnp.where(causal, seg, -jnp.inf))
    cb = jnp.einsum("bctgn,bcsgn->bctsg", Cc, Bc)
    m = cb[..., None] * decay * dtc[:, :, None]
    y_diag = jnp.einsum("bctsgh,bcsghp->bctghp", m, x)
    last = cum[:, :, -1]
    xw = x * (jnp.exp(last[:, :, None] - cum) * dtc)[..., None]
    states = jnp.einsum("bcsgn,bcsghp->bcghpn", Bc, xw)

    def step(h, inp):
        st, dec = inp
        return dec[..., None, None] * h + st, h

    h_init = h0.astype(jnp.float32).reshape(b, SSD_GROUPS, SSD_HPG, SSD_HEAD_DIM, SSD_STATE)
    h_final, h_prev = lax.scan(step, h_init, (jnp.moveaxis(states, 1, 0), jnp.moveaxis(jnp.exp(last), 1, 0)))
    h_prev = jnp.moveaxis(h_prev, 0, 1)
    y_off = jnp.einsum("bctgn,bcghpn->bctghp", Cc, h_prev) * jnp.exp(cum)[..., None]
    y = (y_diag + y_off).reshape(b, T, SSD_HEADS, SSD_HEAD_DIM)
    return y, h_final.reshape(b, SSD_HEADS, SSD_HEAD_DIM, SSD_STATE)


def hybrid_layer(x, conv_prev, ssm_prev, k_prev, v_prev, logf_prev,
                 g_mix, w_in, f_bias, g_q, g_k, w_conv, b_conv, dt_bias, a_log, d_skip, g_ssd,
                 w_out, g_ffn, w_gate, w_up, w_down):
    b, T, _ = x.shape
    past = k_prev.shape[1]
    h = rmsnorm(x, g_mix)
    proj = h @ w_in
    offs = [FOX_DIM, 2 * FOX_DIM, 3 * FOX_DIM, 3 * FOX_DIM + FOX_HEADS,
            3 * FOX_DIM + FOX_HEADS + SSD_DIM, 3 * FOX_DIM + FOX_HEADS + SSD_DIM + CONV_DIM]
    q, k, v, f_raw, z, xbc, dt_raw = jnp.split(proj, offs, axis=-1)
    q = rmsnorm(q.reshape(b, T, FOX_HEADS, FOX_HEAD_DIM), g_q)
    k = rmsnorm(k.reshape(b, T, FOX_HEADS, FOX_HEAD_DIM), g_k)
    v = v.reshape(b, T, FOX_HEADS, FOX_HEAD_DIM)
    logf = jax.nn.log_sigmoid(f_raw.astype(jnp.float32) + f_bias)
    k_all = jnp.concatenate([k_prev.astype(k.dtype), k], axis=1)
    v_all = jnp.concatenate([v_prev.astype(v.dtype), v], axis=1)
    logf_all = jnp.concatenate([logf_prev.astype(jnp.float32), logf], axis=1)
    f_all = jnp.cumsum(logf_all, axis=1)
    qpos = past + jnp.arange(T, dtype=jnp.int32)
    kpos = jnp.arange(past + T, dtype=jnp.int32)
    attn = fox_attention(q, k_all, v_all, f_all[:, past:], f_all, qpos, kpos)
    xbc_all = jnp.concatenate([conv_prev.astype(xbc.dtype), xbc], axis=1)
    u = jax.nn.silu(causal_dwconv(xbc_all, w_conv, b_conv))
    xs, Bm, Cm = jnp.split(u, [SSD_DIM, SSD_DIM + SSD_GROUPS * SSD_STATE], axis=-1)
    xs = xs.reshape(b, T, SSD_HEADS, SSD_HEAD_DIM)
    Bm = Bm.reshape(b, T, SSD_GROUPS, SSD_STATE)
    Cm = Cm.reshape(b, T, SSD_GROUPS, SSD_STATE)
    dt = jax.nn.softplus(dt_raw.astype(jnp.float32) + dt_bias)
    A = -jnp.exp(a_log.astype(jnp.float32))
    y, h_last = ssd_scan(xs, dt, A, Bm, Cm, ssm_prev)
    y = y + d_skip[:, None] * xs
    y = gated_rmsnorm(y.reshape(b, T, SSD_DIM), z, g_ssd)
    mix = jnp.concatenate([attn.reshape(b, T, FOX_DIM).astype(x.dtype), y.astype(x.dtype)], axis=-1) @ w_out
    x = x + mix
    hf = rmsnorm(x, g_ffn)
    x = x + ((jax.nn.silu(hf @ w_gate) * (hf @ w_up)) @ w_down).astype(x.dtype)
    return x, xbc_all[:, -(CONV_W - 1):], h_last, k, v, logf


def setup_inputs(seed: int = 0) -> dict:
    key = jax.random.key(seed)
    ks = jax.random.split(key, 24)
    f32 = jnp.float32

    def nrm(k, shape, scale):
        return jax.random.normal(k, shape, f32) * scale

    x_prompt = nrm(ks[0], (BATCH, SEQ, D_MODEL), 1.0)
    x_sample = nrm(ks[1], (DEC_BATCH, DEC_SEQ, D_MODEL), 1.0)
    cache_conv = nrm(ks[2], (DEPTH, DEC_BATCH, CONV_W - 1, CONV_DIM), 1.0)
    state_ssm = nrm(ks[3], (DEPTH, DEC_BATCH, SSD_HEADS, SSD_HEAD_DIM, SSD_STATE), 0.1)
    cache_fox_k = nrm(ks[4], (DEPTH, DEC_BATCH, PAST_LEN, FOX_HEADS, FOX_HEAD_DIM), 1.0)
    cache_fox_v = nrm(ks[5], (DEPTH, DEC_BATCH, PAST_LEN, FOX_HEADS, FOX_HEAD_DIM), 1.0)
    cache_fox_logf = jax.nn.log_sigmoid(FORGET_BIAS_INIT + nrm(ks[6], (DEPTH, DEC_BATCH, PAST_LEN, FOX_HEADS), 1.0))
    g_mix = 1.0 + nrm(ks[7], (DEPTH, D_MODEL), 0.02)
    w_in = nrm(ks[8], (DEPTH, D_MODEL, IN_DIM), D_MODEL ** -0.5)
    f_bias = FORGET_BIAS_INIT + nrm(ks[9], (DEPTH, FOX_HEADS), 0.1)
    g_q = 1.0 + nrm(ks[10], (DEPTH, FOX_HEAD_DIM), 0.02)
    g_k = 1.0 + nrm(ks[11], (DEPTH, FOX_HEAD_DIM), 0.02)
    w_conv = nrm(ks[12], (DEPTH, CONV_W, CONV_DIM), CONV_W ** -0.5)
    b_conv = nrm(ks[13], (DEPTH, CONV_DIM), 0.02)
    dt0 = jnp.exp(jax.random.uniform(ks[14], (DEPTH, SSD_HEADS), f32, math.log(1e-3), math.log(1e-1)))
    dt_bias = dt0 + jnp.log(-jnp.expm1(-dt0))
    a_log = jnp.log(jax.random.uniform(ks[15], (DEPTH, SSD_HEADS), f32, 1.0, 16.0))
    d_skip = 1.0 + nrm(ks[16], (DEPTH, SSD_HEADS), 0.02)
    g_ssd = 1.0 + nrm(ks[17], (DEPTH, SSD_DIM), 0.02)
    w_out = nrm(ks[18], (DEPTH, MIX_DIM, D_MODEL), MIX_DIM ** -0.5)
    g_ffn = 1.0 + nrm(ks[19], (DEPTH, D_MODEL), 0.02)
    w_gate = nrm(ks[20], (DEPTH, D_MODEL, D_FF), D_MODEL ** -0.5)
    w_up = nrm(ks[21], (DEPTH, D_MODEL, D_FF), D_MODEL ** -0.5)
    w_down = nrm(ks[22], (DEPTH, D_FF, D_MODEL), D_FF ** -0.5)
    return {"x_prompt": x_prompt, "x_sample": x_sample,
            "cache_conv": cache_conv, "state_ssm": state_ssm,
            "cache_fox_k": cache_fox_k, "cache_fox_v": cache_fox_v, "cache_fox_logf": cache_fox_logf,
            "g_mix": g_mix, "w_in": w_in, "f_bias": f_bias, "g_q": g_q, "g_k": g_k,
            "w_conv": w_conv, "b_conv": b_conv, "dt_bias": dt_bias, "a_log": a_log,
            "d_skip": d_skip, "g_ssd": g_ssd, "w_out": w_out, "g_ffn": g_ffn,
            "w_gate": w_gate, "w_up": w_up, "w_down": w_down}


def reference(x_prompt, x_sample, cache_conv, state_ssm, cache_fox_k, cache_fox_v, cache_fox_logf,
              g_mix, w_in, f_bias, g_q, g_k, w_conv, b_conv, dt_bias, a_log, d_skip, g_ssd,
              w_out, g_ffn, w_gate, w_up, w_down):
    yp, ys = x_prompt, x_sample
    bp = x_prompt.shape[0]
    p_conv, p_ssm, p_k, p_v, p_f = [], [], [], [], []
    s_conv, s_ssm, s_k, s_v, s_f = [], [], [], [], []
    for i in range(DEPTH):
        wts = (g_mix[i], w_in[i], f_bias[i], g_q[i], g_k[i], w_conv[i], b_conv[i], dt_bias[i],
               a_log[i], d_skip[i], g_ssd[i], w_out[i], g_ffn[i], w_gate[i], w_up[i], w_down[i])
        yp, c, h, k, v, f = hybrid_layer(
            yp,
            jnp.zeros((bp, CONV_W - 1, CONV_DIM), yp.dtype),
            jnp.zeros((bp, SSD_HEADS, SSD_HEAD_DIM, SSD_STATE), jnp.float32),
            jnp.zeros((bp, 0, FOX_HEADS, FOX_HEAD_DIM), yp.dtype),
            jnp.zeros((bp, 0, FOX_HEADS, FOX_HEAD_DIM), yp.dtype),
            jnp.zeros((bp, 0, FOX_HEADS), jnp.float32),
            *wts)
        p_conv.append(c); p_ssm.append(h); p_k.append(k); p_v.append(v); p_f.append(f)
        ys, c, h, k, v, f = hybrid_layer(
            ys, cache_conv[i], state_ssm[i], cache_fox_k[i], cache_fox_v[i], cache_fox_logf[i], *wts)
        s_conv.append(c); s_ssm.append(h); s_k.append(k); s_v.append(v); s_f.append(f)
    return (yp, ys,
            jnp.stack(p_conv), jnp.stack(p_ssm), jnp.stack(p_k), jnp.stack(p_v), jnp.stack(p_f),
            jnp.stack(s_conv), jnp.stack(s_ssm), jnp.stack(s_k), jnp.stack(s_v), jnp.stack(s_f))
```

```python
import functools

import jax
import jax.numpy as jnp
from jax import lax
from jax.experimental import pallas as pl
from jax.experimental.pallas import tpu as pltpu

F32 = jnp.float32
BF16 = jnp.bfloat16

EPS = 1e-6
HEAD_DIM = 64
N_HEADS = 16
FOX_DIM = 1024
SSD_DIM = 1024
SSD_GROUPS = 2
SSD_STATE = 128
GROUP_DIM = SSD_DIM // SSD_GROUPS
CONV_W = 4
CONV_DIM = SSD_DIM + 2 * SSD_GROUPS * SSD_STATE
LANES = 128
SUBLANES = 8
MXU_DIM = 256
NEG_BIG = -1e30
VMEM_LIMIT = 56 * 1024 * 1024

COL_Q, COL_K, COL_V, COL_Z = 0, FOX_DIM, 2 * FOX_DIM, 3 * FOX_DIM
COL_XBC = COL_Z + SSD_DIM
COL_F = COL_XBC + CONV_DIM
COL_DT = COL_F + LANES
IN_COLS = COL_DT + LANES


def _dot(a, b):
    return jnp.dot(a, b, preferred_element_type=F32)


def _dot_nt(a, b):
    return lax.dot_general(a, b, (((1,), (1,)), ((), ())), preferred_element_type=F32)


def _dot_tn(a, b):
    return lax.dot_general(a, b, (((0,), (0,)), ((), ())), preferred_element_type=F32)


def _split3(x):
    x1 = x.astype(BF16)
    r = x - x1.astype(F32)
    x2 = r.astype(BF16)
    r = r - x2.astype(F32)
    return x1, x2, r.astype(BF16)


def _sel_dot(sel, x):
    x1, x2, x3 = _split3(x)
    return _dot(sel, x1) + _dot(sel, x2) + _dot(sel, x3)


def _dot_sel(x, sel):
    x1, x2, x3 = _split3(x)
    return _dot(x1, sel) + _dot(x2, sel) + _dot(x3, sel)


def _sel_dot_nt(sel, x):
    x1, x2, x3 = _split3(x)
    return _dot_nt(sel, x1) + _dot_nt(sel, x2) + _dot_nt(sel, x3)


def _sigmoid(x):
    return 1.0 / (1.0 + jnp.exp(-x))


def _softplus(x):
    return jnp.maximum(x, 0.0) + jnp.log1p(jnp.exp(-jnp.abs(x)))


def _resident(shape):
    nd = len(shape)
    return pl.BlockSpec(shape, lambda *_: (0,) * nd, pipeline_mode=pl.Buffered(1))


def _params(sem):
    return pltpu.CompilerParams(dimension_semantics=sem, vmem_limit_bytes=VMEM_LIMIT)


def _inproj_kernel(x_ref, gmix_ref, w_ref, e_ref, gq_ref, gk_ref, fb_ref, dtb_ref,
                   q_ref, k_ref, kb_ref, v_ref, vb_ref, z_ref, xbc_ref, logf_ref, dt_ref):
    x = x_ref[...]
    ms = jnp.mean(x * x, axis=-1, keepdims=True)
    h = (x * lax.rsqrt(ms + EPS) * gmix_ref[...]).astype(BF16)
    e = e_ref[...]

    def head_rms(y, g):
        ysq = y * y
        hi = ysq.astype(BF16)
        lo = (ysq - hi.astype(F32)).astype(BF16)
        parts = []
        for c in range(FOX_DIM // MXU_DIM):
            sl = slice(MXU_DIM * c, MXU_DIM * (c + 1))
            parts.append(_dot(hi[:, sl], e) + _dot(lo[:, sl], e))
        ss = jnp.concatenate(parts, axis=-1)
        return y * lax.rsqrt(ss * (1.0 / HEAD_DIM) + EPS) * g

    q = _dot(h, w_ref[:, COL_Q:COL_Q + FOX_DIM])
    q_ref[...] = (head_rms(q, gq_ref[...]) * (HEAD_DIM ** -0.5)).astype(BF16)
    k = head_rms(_dot(h, w_ref[:, COL_K:COL_K + FOX_DIM]), gk_ref[...])
    k_ref[...] = k
    kb_ref[...] = k.astype(BF16)
    v = _dot(h, w_ref[:, COL_V:COL_V + FOX_DIM])
    v_ref[...] = v
    vb_ref[...] = v.astype(BF16)
    z_ref[...] = _dot(h, w_ref[:, COL_Z:COL_Z + SSD_DIM])
    xbc_ref[...] = _dot(h, w_ref[:, COL_XBC:COL_XBC + CONV_DIM])
    f_raw = _dot(h, w_ref[:, COL_F:COL_F + LANES])[:, :N_HEADS]
    logf_ref[...] = -_softplus(-(f_raw + fb_ref[...]))
    dt_raw = _dot(h, w_ref[:, COL_DT:COL_DT + LANES])[:, :N_HEADS]
    dt_ref[...] = _softplus(dt_raw + dtb_ref[...])


def _inproj(x2d, g_mix, w_all, g_q, g_k, f_bias, dt_bias):
    n, d = x2d.shape
    tm = min(256, n)
    assert n % tm == 0
    blk = jnp.arange(MXU_DIM) // HEAD_DIM
    e = (blk[:, None] == blk[None, :]).astype(BF16)
    row = lambda w: pl.BlockSpec((tm, w), lambda i: (i, 0))
    out_shapes = (
        jax.ShapeDtypeStruct((n, FOX_DIM), BF16),
        jax.ShapeDtypeStruct((n, FOX_DIM), F32),
        jax.ShapeDtypeStruct((n, FOX_DIM), BF16),
        jax.ShapeDtypeStruct((n, FOX_DIM), F32),
        jax.ShapeDtypeStruct((n, FOX_DIM), BF16),
        jax.ShapeDtypeStruct((n, SSD_DIM), F32),
        jax.ShapeDtypeStruct((n, CONV_DIM), F32),
        jax.ShapeDtypeStruct((n, N_HEADS), F32),
        jax.ShapeDtypeStruct((n, N_HEADS), F32),
    )
    return pl.pallas_call(
        _inproj_kernel,
        grid=(n // tm,),
        in_specs=[row(d), _resident((1, d)), _resident((d, IN_COLS)), _resident((MXU_DIM, MXU_DIM)),
                  _resident((1, FOX_DIM)), _resident((1, FOX_DIM)),
                  _resident((1, N_HEADS)), _resident((1, N_HEADS))],
        out_specs=(row(FOX_DIM), row(FOX_DIM), row(FOX_DIM), row(FOX_DIM), row(FOX_DIM),
                   row(SSD_DIM), row(CONV_DIM), row(N_HEADS), row(N_HEADS)),
        out_shape=out_shapes,
        compiler_params=_params(("arbitrary",)),
        name="inproj",
    )(x2d, g_mix.reshape(1, d), w_all, e,
      jnp.tile(g_q, N_HEADS).reshape(1, FOX_DIM), jnp.tile(g_k, N_HEADS).reshape(1, FOX_DIM),
      f_bias.reshape(1, N_HEADS), dt_bias.reshape(1, N_HEADS))


def _cumsum_kernel(x_ref, hi_ref, mid_ref, lo_ref, *, blk, nblk):
    r = lax.broadcasted_iota(jnp.int32, (blk, blk), 0)
    c = lax.broadcasted_iota(jnp.int32, (blk, blk), 1)
    ltri = jnp.where(r >= c, 1.0, 0.0).astype(BF16)

    def body(i, carry):
        off = pl.multiple_of(i * blk, blk)
        f = _sel_dot(ltri, x_ref[pl.ds(off, blk), :]) + carry
        f1, f2, f3 = _split3(f)
        hi_ref[pl.ds(off, blk), :] = f1
        mid_ref[pl.ds(off, blk), :] = f2
        lo_ref[pl.ds(off, blk), :] = f3
        return f[blk - 1:blk, :]

    lax.fori_loop(0, nblk, body, jnp.zeros((1, x_ref.shape[1]), F32))


def _cumsum_time(x_tc):
    t, c = x_tc.shape
    blk = 64
    assert t % blk == 0
    out = jax.ShapeDtypeStruct((t, c), BF16)
    return pl.pallas_call(
        functools.partial(_cumsum_kernel, blk=blk, nblk=t // blk),
        grid=(1,),
        in_specs=[_resident((t, c))],
        out_specs=(pl.BlockSpec((t, c), lambda i: (0, 0)),) * 3,
        out_shape=(out, out, out),
        compiler_params=_params(("arbitrary",)),
        name="cumsum_time",
    )(x_tc)


N_EXTRA = 3
Q_EXTRA_SPLIT = 2 * N_EXTRA


def _attn_kernel(q_ref, qx_ref, kc_ref, v_ref, o_ref, qs_ref, m_ref, l_ref, acc_ref, *, tq, tk, past):
    i = pl.program_id(2)
    q2 = q_ref[0]
    qx2 = qx_ref[0]
    lane = lax.broadcasted_iota(jnp.int32, (tq, LANES), 1)
    zero = jnp.zeros_like(q2)
    qs_ref[0:tq, 0:LANES] = jnp.where(lane < HEAD_DIM, q2, zero)
    qs_ref[0:tq, LANES:2 * LANES] = jnp.where(lane < Q_EXTRA_SPLIT, qx2, zero)
    qs_ref[tq:2 * tq, 0:LANES] = jnp.where(lane >= HEAD_DIM, q2, zero)
    qs_ref[tq:2 * tq, LANES:2 * LANES] = jnp.where(lane >= Q_EXTRA_SPLIT, qx2, zero)
    m_ref[...] = jnp.full(m_ref.shape, NEG_BIG, F32)
    l_ref[...] = jnp.zeros(l_ref.shape, F32)
    acc_ref[...] = jnp.zeros(acc_ref.shape, F32)

    q_lo = past + i * tq
    n_full = (q_lo + 1) // tk
    n_total = (q_lo + tq - 1) // tk + 1

    def step(j, masked):
        off = pl.multiple_of(j * tk, tk)
        kc = kc_ref[0, pl.ds(off, tk), :]
        vc = v_ref[0, pl.ds(off, tk), :]
        s = _dot_nt(qs_ref[...], kc)
        if masked:
            row = lax.broadcasted_iota(jnp.int32, (2 * tq, tk), 0)
            col = lax.broadcasted_iota(jnp.int32, (2 * tq, tk), 1)
            qpos = q_lo + jnp.where(row >= tq, row - tq, row)
            s = jnp.where(off + col <= qpos, s, NEG_BIG)
        m_prev = m_ref[...]
        m_new = jnp.maximum(m_prev, jnp.max(s, axis=1, keepdims=True))
        alpha = jnp.exp(m_prev - m_new)
        p = jnp.exp(s - m_new)
        l_ref[...] = alpha * l_ref[...] + jnp.sum(p, axis=1, keepdims=True)
        acc_ref[...] = alpha * acc_ref[...] + _dot(p.astype(BF16), vc)
        m_ref[...] = m_new

    def full_body(j, carry):
        step(j, False)
        return carry

    def masked_body(j, carry):
        step(j, True)
        return carry

    lax.fori_loop(0, n_full, full_body, 0)
    lax.fori_loop(n_full, n_total, masked_body, 0)

    o = acc_ref[...] * (1.0 / l_ref[...])
    o_ref[0] = jnp.where(lane < HEAD_DIM, o[0:tq], o[tq:2 * tq]).astype(BF16)


def _attention(q, qx, kcat, v, *, past, tq, tk):
    b, t, _ = q.shape
    tkeys = kcat.shape[1]
    npairs = N_HEADS // 2
    assert t % tq == 0 and tkeys % tk == 0
    return pl.pallas_call(
        functools.partial(_attn_kernel, tq=tq, tk=tk, past=past),
        grid=(b, npairs, t // tq),
        in_specs=[pl.BlockSpec((1, tq, LANES), lambda bi, p, i: (bi, i, p)),
                  pl.BlockSpec((1, tq, LANES), lambda bi, p, i: (bi, i, p)),
                  pl.BlockSpec((1, tkeys, 2 * LANES), lambda bi, p, i: (bi, 0, p)),
                  pl.BlockSpec((1, tkeys, LANES), lambda bi, p, i: (bi, 0, p))],
        out_specs=pl.BlockSpec((1, tq, LANES), lambda bi, p, i: (bi, i, p)),
        out_shape=jax.ShapeDtypeStruct((b, t, FOX_DIM), BF16),
        scratch_shapes=[pltpu.VMEM((2 * tq, 2 * LANES), BF16),
                        pltpu.VMEM((2 * tq, 1), F32),
                        pltpu.VMEM((2 * tq, 1), F32),
                        pltpu.VMEM((2 * tq, LANES), F32)],
        compiler_params=_params(("arbitrary", "arbitrary", "arbitrary")),
        name="fox_attention",
    )(q, qx, kcat, v)


def _bias_columns(hi, mid, lo, b):
    t = hi.shape[0]
    pieces = jnp.stack([hi, mid, lo], axis=-1).reshape(t, b, N_HEADS // 2, 2, N_EXTRA)
    pieces = jnp.transpose(pieces, (1, 0, 2, 3, 4))
    even, odd = pieces[:, :, :, 0, :], pieces[:, :, :, 1, :]
    ones = jnp.ones_like(even)
    pad = jnp.zeros(even.shape[:-1] + (LANES - 4 * N_EXTRA,), BF16)
    kx = jnp.concatenate([ones, -even, -odd, ones, pad], axis=-1)
    qx = jnp.concatenate([even, ones, ones, odd, pad], axis=-1)
    return kx, qx


def _ssd_kernel(xbc_ref, z_ref, dt_ref, cprev_ref, sprev_ref, wconv_ref, bconv_ref, alog_ref,
                dskip_ref, gssd_ref, rexp_ref, rexpt_ref,
                y_ref, hfin_ref, ext_ref, state_ref, *, L):
    c = pl.program_id(1)

    @pl.when(c == 0)
    def _():
        state_ref[...] = sprev_ref[0]
        ext_ref[0:SUBLANES, :] = cprev_ref[0]

    x_tile = xbc_ref[0]
    ext_ref[SUBLANES:SUBLANES + L, :] = x_tile
    w = wconv_ref[...]
    pre = bconv_ref[...]
    for j in range(CONV_W):
        start = SUBLANES - (CONV_W - 1) + j
        pre = pre + ext_ref[start:start + L, :] * w[j:j + 1, :]
    ext_ref[0:SUBLANES, :] = x_tile[L - SUBLANES:L, :]
    u = pre * _sigmoid(pre)
    xs = u[:, 0:SSD_DIM]
    bmat = u[:, SSD_DIM:SSD_DIM + SSD_GROUPS * SSD_STATE].astype(BF16)
    cmat = u[:, SSD_DIM + SSD_GROUPS * SSD_STATE:].astype(BF16)

    dt = dt_ref[0]
    a = dt * (-jnp.exp(alog_ref[...]))
    r = lax.broadcasted_iota(jnp.int32, (L, L), 0)
    cc = lax.broadcasted_iota(jnp.int32, (L, L), 1)
    causal = r >= cc
    ltri = jnp.where(causal, 1.0, 0.0).astype(BF16)
    cum = _sel_dot(ltri, a)
    e_r = lax.broadcasted_iota(jnp.int32, (N_HEADS, N_HEADS), 0)
    e_c = lax.broadcasted_iota(jnp.int32, (N_HEADS, N_HEADS), 1)
    eye = jnp.where(e_r == e_c, 1.0, 0.0).astype(BF16)
    cum_t = _sel_dot_nt(eye, cum)
    last = cum[L - 1:L, :]

    rexp = rexp_ref[...]
    dtx = _dot_sel(dt, rexp)
    wendx = _dot_sel(jnp.exp(last - cum), rexp)
    ecumx = _dot_sel(jnp.exp(cum), rexp)
    xdt = xs * dtx
    xdt_b = xdt.astype(BF16)
    xw_b = (xdt * wendx).astype(BF16)
    elast = jnp.exp(cum_t[:, L - 1:L])
    elast_b = jnp.broadcast_to(elast, (N_HEADS, SSD_STATE))
    rexpt = rexpt_ref[...]
    lane = lax.broadcasted_iota(jnp.int32, (L, LANES), 1)

    heads_per_group = N_HEADS // SSD_GROUPS
    ys = []
    for g in range(SSD_GROUPS):
        gs = slice(GROUP_DIM * g, GROUP_DIM * (g + 1))
        cg = cmat[:, SSD_STATE * g:SSD_STATE * (g + 1)]
        bg = bmat[:, SSD_STATE * g:SSD_STATE * (g + 1)]
        cb = _dot_nt(cg, bg)
        st = state_ref[gs, :]
        yoff = _dot_nt(cg, st.astype(BF16))
        for pr in range(heads_per_group // 2):
            ps = slice(GROUP_DIM * g + LANES * pr, GROUP_DIM * g + LANES * (pr + 1))
            slab = xdt_b[:, ps]
            res = []
            for hh in range(2):
                hd = heads_per_group * g + 2 * pr + hh
                seg = cum[:, hd:hd + 1] - cum_t[hd:hd + 1, :]
                dec = jnp.exp(jnp.where(causal, seg, -jnp.inf))
                res.append(_dot((cb * dec).astype(BF16), slab))
            ydiag = jnp.where(lane < HEAD_DIM, res[0], res[1])
            ys.append(ydiag + yoff[:, LANES * pr:LANES * (pr + 1)] * ecumx[:, ps])
        snew = _dot_tn(xw_b[:, gs], bg)
        scale = _sel_dot(rexpt[gs, :], elast_b)
        state_ref[gs, :] = scale * st + snew

    y = jnp.concatenate(ys, axis=-1) + dskip_ref[...] * xs
    zt = z_ref[0]
    ug = y * (zt * _sigmoid(zt))
    outs = []
    for g in range(SSD_GROUPS):
        ugg = ug[:, GROUP_DIM * g:GROUP_DIM * (g + 1)]
        outs.append(ugg * lax.rsqrt(jnp.mean(ugg * ugg, axis=-1, keepdims=True) + EPS))
    y_ref[0] = (jnp.concatenate(outs, axis=-1) * gssd_ref[...]).astype(BF16)

    @pl.when(c == pl.num_programs(1) - 1)
    def _():
        hfin_ref[0] = state_ref[...]


def _ssd(xbc, z, dt, conv_prev8, ssm_prev, w_conv, b_conv, a_log, d_skip, g_ssd, *, L):
    b, t, _ = xbc.shape
    assert t % L == 0 and L % SUBLANES == 0
    head_of_lane = jnp.arange(SSD_DIM) // HEAD_DIM
    rexp = (jnp.arange(N_HEADS)[:, None] == head_of_lane[None, :]).astype(BF16)
    tile = lambda w: pl.BlockSpec((1, L, w), lambda bi, ci: (bi, ci, 0))
    per_b = lambda s: pl.BlockSpec((1,) + s, lambda bi, ci: (bi, 0, 0))
    return pl.pallas_call(
        functools.partial(_ssd_kernel, L=L),
        grid=(b, t // L),
        in_specs=[tile(CONV_DIM), tile(SSD_DIM), tile(N_HEADS),
                  per_b((SUBLANES, CONV_DIM)), per_b((SSD_DIM, SSD_STATE)),
                  _resident((CONV_W, CONV_DIM)), _resident((1, CONV_DIM)), _resident((1, N_HEADS)),
                  _resident((1, SSD_DIM)), _resident((1, SSD_DIM)),
                  _resident((N_HEADS, SSD_DIM)), _resident((SSD_DIM, N_HEADS))],
        out_specs=(tile(SSD_DIM), per_b((SSD_DIM, SSD_STATE))),
        out_shape=(jax.ShapeDtypeStruct((b, t, SSD_DIM), BF16),
                   jax.ShapeDtypeStruct((b, SSD_DIM, SSD_STATE), F32)),
        scratch_shapes=[pltpu.VMEM((L + SUBLANES, CONV_DIM), F32),
                        pltpu.VMEM((SSD_DIM, SSD_STATE), F32)],
        compiler_params=_params(("arbitrary", "arbitrary")),
        name="ssd_scan",
    )(xbc, z, dt, conv_prev8, ssm_prev, w_conv, b_conv.reshape(1, CONV_DIM), a_log.reshape(1, N_HEADS),
      jnp.repeat(d_skip, HEAD_DIM).reshape(1, SSD_DIM), g_ssd.reshape(1, SSD_DIM), rexp, rexp.T)


def _outproj_kernel(x_ref, attn_ref, y_ref, wo_ref, o_ref):
    mix = _dot(attn_ref[...], wo_ref[0:FOX_DIM, :]) + _dot(y_ref[...], wo_ref[FOX_DIM:, :])
    o_ref[...] = x_ref[...] + mix


def _outproj(x2d, attn, y, w_out):
    n, d = x2d.shape
    tm = min(512, n)
    assert n % tm == 0
    row = lambda w: pl.BlockSpec((tm, w), lambda i: (i, 0))
    return pl.pallas_call(
        _outproj_kernel,
        grid=(n // tm,),
        in_specs=[row(d), row(FOX_DIM), row(SSD_DIM), _resident((FOX_DIM + SSD_DIM, d))],
        out_specs=row(d),
        out_shape=jax.ShapeDtypeStruct((n, d), F32),
        compiler_params=_params(("arbitrary",)),
        name="outproj",
    )(x2d, attn, y, w_out)


def _ffn_kernel(x_ref, g_ref, wg_ref, wu_ref, wd_ref, o_ref, hf_ref):
    j = pl.program_id(1)

    @pl.when(j == 0)
    def _():
        x = x_ref[...]
        ms = jnp.mean(x * x, axis=-1, keepdims=True)
        hf_ref[...] = (x * lax.rsqrt(ms + EPS) * g_ref[...]).astype(BF16)
        o_ref[...] = x

    hf = hf_ref[...]
    gate = _dot(hf, wg_ref[...])
    act = (gate * _sigmoid(gate) * _dot(hf, wu_ref[...])).astype(BF16)
    o_ref[...] += _dot(act, wd_ref[...])


def _ffn(x2d, g_ffn, w_gate, w_up, w_down):
    n, d = x2d.shape
    dff = w_gate.shape[1]
    tm, tf = min(512, n), 512
    assert n % tm == 0 and dff % tf == 0
    return pl.pallas_call(
        _ffn_kernel,
        grid=(n // tm, dff // tf),
        in_specs=[pl.BlockSpec((tm, d), lambda i, j: (i, 0)),
                  _resident((1, d)),
                  pl.BlockSpec((d, tf), lambda i, j: (0, j)),
                  pl.BlockSpec((d, tf), lambda i, j: (0, j)),
                  pl.BlockSpec((tf, d), lambda i, j: (j, 0))],
        out_specs=pl.BlockSpec((tm, d), lambda i, j: (i, 0)),
        out_shape=jax.ShapeDtypeStruct((n, d), F32),
        scratch_shapes=[pltpu.VMEM((tm, d), BF16)],
        compiler_params=_params(("arbitrary", "arbitrary")),
        name="ffn",
    )(x2d, g_ffn.reshape(1, d), w_gate, w_up, w_down)


def _pack_w_in(w_in):
    d = w_in.shape[0]
    o_f = 3 * FOX_DIM
    o_z = o_f + N_HEADS
    o_xbc = o_z + SSD_DIM
    o_dt = o_xbc + CONV_DIM
    pad = jnp.zeros((d, LANES - N_HEADS), w_in.dtype)
    return jnp.concatenate([w_in[:, :o_f], w_in[:, o_z:o_xbc], w_in[:, o_xbc:o_dt],
                            w_in[:, o_f:o_z], pad, w_in[:, o_dt:], pad], axis=1).astype(BF16)


def _layer(x, conv_prev, ssm_prev, k_prev, v_prev, logf_prev, wts, *, attn_tq, attn_tk, ssd_chunk):
    (g_mix, w_all, f_bias, g_q, g_k, w_conv, b_conv, dt_bias, a_log, d_skip, g_ssd,
     w_out, g_ffn, w_gate, w_up, w_down) = wts
    b, t, d = x.shape
    n = b * t
    x2d = x.reshape(n, d)
    qb, k, kb, v, vb, z, xbc, logf, dt = _inproj(x2d, g_mix, w_all, g_q, g_k, f_bias, dt_bias)

    past = 0 if k_prev is None else k_prev.shape[1]
    kb3, vb3, logf3 = kb.reshape(b, t, FOX_DIM), vb.reshape(b, t, FOX_DIM), logf.reshape(b, t, N_HEADS)
    if past:
        kb3 = jnp.concatenate([k_prev.reshape(b, past, FOX_DIM).astype(BF16), kb3], axis=1)
        vb3 = jnp.concatenate([v_prev.reshape(b, past, FOX_DIM).astype(BF16), vb3], axis=1)
        logf3 = jnp.concatenate([logf_prev.astype(F32), logf3], axis=1)
    tkeys = past + t
    logf_tc = jnp.transpose(logf3, (1, 0, 2)).reshape(tkeys, b * N_HEADS)
    kx, qx = _bias_columns(*_cumsum_time(logf_tc), b)
    qx = qx[:, past:].reshape(b, t, FOX_DIM)
    kcat = jnp.concatenate([kb3.reshape(b, tkeys, N_HEADS // 2, LANES), kx], axis=-1)
    kcat = kcat.reshape(b, tkeys, 2 * FOX_DIM)
    tk = attn_tk
    if tkeys % tk:
        tk = -(-tkeys // LANES) * LANES
        kcat = jnp.pad(kcat, ((0, 0), (0, tk - tkeys), (0, 0)))
        vb3 = jnp.pad(vb3, ((0, 0), (0, tk - tkeys), (0, 0)))
    attn = _attention(qb.reshape(b, t, FOX_DIM), qx, kcat, vb3, past=past, tq=attn_tq, tk=tk)

    conv_prev8 = jnp.pad(conv_prev.astype(F32), ((0, 0), (SUBLANES - (CONV_W - 1), 0), (0, 0)))
    y, h_last = _ssd(xbc.reshape(b, t, CONV_DIM), z.reshape(b, t, SSD_DIM), dt.reshape(b, t, N_HEADS),
                     conv_prev8, ssm_prev.astype(F32).reshape(b, SSD_DIM, SSD_STATE),
                     w_conv, b_conv, a_log, d_skip, g_ssd, L=ssd_chunk)

    x1 = _outproj(x2d, attn.reshape(n, FOX_DIM), y.reshape(n, SSD_DIM), w_out)
    out = _ffn(x1, g_ffn, w_gate, w_up, w_down)

    xbc_all = jnp.concatenate([conv_prev.astype(F32), xbc.reshape(b, t, CONV_DIM)], axis=1)
    return (out.reshape(b, t, d), xbc_all[:, -(CONV_W - 1):],
            h_last.reshape(b, N_HEADS, HEAD_DIM, SSD_STATE),
            k.reshape(b, t, N_HEADS, HEAD_DIM), v.reshape(b, t, N_HEADS, HEAD_DIM), logf3[:, past:])


def kernel(x_prompt, x_sample, cache_conv, state_ssm, cache_fox_k, cache_fox_v, cache_fox_logf, g_mix, w_in, f_bias, g_q, g_k, w_conv, b_conv, dt_bias, a_log, d_skip, g_ssd, w_out, g_ffn, w_gate, w_up, w_down):
    depth = g_mix.shape[0]
    yp, ys = x_prompt, x_sample
    bp = x_prompt.shape[0]
    outs_p = [[] for _ in range(5)]
    outs_s = [[] for _ in range(5)]
    for i in range(depth):
        wts = (g_mix[i], _pack_w_in(w_in[i]), f_bias[i], g_q[i], g_k[i], w_conv[i], b_conv[i], dt_bias[i],
               a_log[i], d_skip[i], g_ssd[i], w_out[i].astype(BF16), g_ffn[i],
               w_gate[i].astype(BF16), w_up[i].astype(BF16), w_down[i].astype(BF16))
        tp = yp.shape[1]
        yp, *rest = _layer(
            yp, jnp.zeros((bp, CONV_W - 1, CONV_DIM), F32),
            jnp.zeros((bp, N_HEADS, HEAD_DIM, SSD_STATE), F32), None, None, None, wts,
            attn_tq=min(256, tp), attn_tk=min(512, tp), ssd_chunk=min(128, tp))
        for lst, val in zip(outs_p, rest):
            lst.append(val)
        tsamp = ys.shape[1]
        ys, *rest = _layer(
            ys, cache_conv[i], state_ssm[i], cache_fox_k[i], cache_fox_v[i], cache_fox_logf[i], wts,
            attn_tq=tsamp, attn_tk=LANES, ssd_chunk=tsamp)
        for lst, val in zip(outs_s, rest):
            lst.append(val)
    return (yp, ys, *[jnp.stack(l) for l in outs_p], *[jnp.stack(l) for l in outs_s])
```

```python
import functools

import jax
import jax.numpy as jnp
import numpy as np
from jax import lax
from jax.experimental import pallas as pl
from jax.experimental.pallas import tpu as pltpu

F32 = jnp.float32
BF16 = jnp.bfloat16

EPS = 1e-6
HEAD_DIM = 64
N_HEADS = 16
FOX_DIM = 1024
SSD_DIM = 1024
SSD_GROUPS = 2
SSD_STATE = 128
GROUP_DIM = SSD_DIM // SSD_GROUPS
CONV_W = 4
CONV_DIM = SSD_DIM + 2 * SSD_GROUPS * SSD_STATE
LANES = 128
SUBLANES = 8
MXU_DIM = 256
NEG_BIG = -1e30
LOG2E = 1.4426950408889634
VMEM_LIMIT = 56 * 1024 * 1024

COL_Q, COL_K, COL_V, COL_Z = 0, FOX_DIM, 2 * FOX_DIM, 3 * FOX_DIM
COL_XBC = COL_Z + SSD_DIM
COL_F = COL_XBC + CONV_DIM
COL_DT = COL_F + LANES
IN_COLS = COL_DT + LANES


def _dot(a, b):
    return jnp.dot(a, b, preferred_element_type=F32)


def _dot_nt(a, b):
    return lax.dot_general(a, b, (((1,), (1,)), ((), ())), preferred_element_type=F32)


def _dot_tn(a, b):
    return lax.dot_general(a, b, (((0,), (0,)), ((), ())), preferred_element_type=F32)


def _split3(x):
    x1 = x.astype(BF16)
    r = x - x1.astype(F32)
    x2 = r.astype(BF16)
    r = r - x2.astype(F32)
    return x1, x2, r.astype(BF16)


def _sel_dot(sel, x):
    x1, x2, x3 = _split3(x)
    return _dot(sel, x1) + _dot(sel, x2) + _dot(sel, x3)


def _dot_sel(x, sel):
    x1, x2, x3 = _split3(x)
    return _dot(x1, sel) + _dot(x2, sel) + _dot(x3, sel)


def _sel_dot_nt(sel, x):
    x1, x2, x3 = _split3(x)
    return _dot_nt(sel, x1) + _dot_nt(sel, x2) + _dot_nt(sel, x3)


def _sigmoid(x):
    return 1.0 / (1.0 + jnp.exp(-x))


def _softplus(x):
    return jnp.maximum(x, 0.0) + jnp.log1p(jnp.exp(-jnp.abs(x)))


def _resident(shape):
    nd = len(shape)
    return pl.BlockSpec(shape, lambda *_: (0,) * nd, pipeline_mode=pl.Buffered(1))


def _params(sem):
    return pltpu.CompilerParams(dimension_semantics=sem, vmem_limit_bytes=VMEM_LIMIT)


def _inproj_kernel(x_ref, gmix_ref, w_ref, e_ref, gq_ref, gk_ref, fb_ref, dtb_ref,
                   q_ref, k_ref, kb_ref, v_ref, vb_ref, z_ref, xbc_ref, logf_ref, dt_ref):
    x = x_ref[...]
    ms = jnp.mean(x * x, axis=-1, keepdims=True)
    h = (x * lax.rsqrt(ms + EPS) * gmix_ref[...]).astype(BF16)
    e = e_ref[...]

    def head_rms(y, g):
        ysq = y * y
        hi = ysq.astype(BF16)
        lo = (ysq - hi.astype(F32)).astype(BF16)
        parts = []
        for c in range(FOX_DIM // MXU_DIM):
            sl = slice(MXU_DIM * c, MXU_DIM * (c + 1))
            parts.append(_dot(hi[:, sl], e) + _dot(lo[:, sl], e))
        ss = jnp.concatenate(parts, axis=-1)
        return y * lax.rsqrt(ss * (1.0 / HEAD_DIM) + EPS) * g

    q = _dot(h, w_ref[:, COL_Q:COL_Q + FOX_DIM])
    q_ref[...] = (head_rms(q, gq_ref[...]) * (HEAD_DIM ** -0.5 * LOG2E)).astype(BF16)
    k = head_rms(_dot(h, w_ref[:, COL_K:COL_K + FOX_DIM]), gk_ref[...])
    k_ref[...] = k
    kb_ref[...] = k.astype(BF16)
    v = _dot(h, w_ref[:, COL_V:COL_V + FOX_DIM])
    v_ref[...] = v
    vb_ref[...] = v.astype(BF16)
    z_ref[...] = _dot(h, w_ref[:, COL_Z:COL_Z + SSD_DIM])
    xbc_ref[...] = _dot(h, w_ref[:, COL_XBC:COL_XBC + CONV_DIM])
    f_raw = _dot(h, w_ref[:, COL_F:COL_F + LANES])[:, :N_HEADS]
    logf_ref[...] = -_softplus(-(f_raw + fb_ref[...]))
    dt_raw = _dot(h, w_ref[:, COL_DT:COL_DT + LANES])[:, :N_HEADS]
    dt_ref[...] = _softplus(dt_raw + dtb_ref[...])


def _inproj(x2d, g_mix, w_all, g_q, g_k, f_bias, dt_bias):
    n, d = x2d.shape
    tm = min(256, n)
    assert n % tm == 0
    blk = jnp.arange(MXU_DIM) // HEAD_DIM
    e = (blk[:, None] == blk[None, :]).astype(BF16)
    row = lambda w: pl.BlockSpec((tm, w), lambda i: (i, 0))
    out_shapes = (
        jax.ShapeDtypeStruct((n, FOX_DIM), BF16),
        jax.ShapeDtypeStruct((n, FOX_DIM), F32),
        jax.ShapeDtypeStruct((n, FOX_DIM), BF16),
        jax.ShapeDtypeStruct((n, FOX_DIM), F32),
        jax.ShapeDtypeStruct((n, FOX_DIM), BF16),
        jax.ShapeDtypeStruct((n, SSD_DIM), F32),
        jax.ShapeDtypeStruct((n, CONV_DIM), F32),
        jax.ShapeDtypeStruct((n, N_HEADS), F32),
        jax.ShapeDtypeStruct((n, N_HEADS), F32),
    )
    return pl.pallas_call(
        _inproj_kernel,
        grid=(n // tm,),
        in_specs=[row(d), _resident((1, d)), _resident((d, IN_COLS)), _resident((MXU_DIM, MXU_DIM)),
                  _resident((1, FOX_DIM)), _resident((1, FOX_DIM)),
                  _resident((1, N_HEADS)), _resident((1, N_HEADS))],
        out_specs=(row(FOX_DIM), row(FOX_DIM), row(FOX_DIM), row(FOX_DIM), row(FOX_DIM),
                   row(SSD_DIM), row(CONV_DIM), row(N_HEADS), row(N_HEADS)),
        out_shape=out_shapes,
        compiler_params=_params(("arbitrary",)),
        name="inproj",
    )(x2d, g_mix.reshape(1, d), w_all, e,
      jnp.tile(g_q, N_HEADS).reshape(1, FOX_DIM), jnp.tile(g_k, N_HEADS).reshape(1, FOX_DIM),
      f_bias.reshape(1, N_HEADS), dt_bias.reshape(1, N_HEADS))


N_EXTRA = 3
Q_EXTRA_SPLIT = 2 * N_EXTRA


def _placement_constants():
    pq = np.zeros((N_EXTRA, N_HEADS, FOX_DIM), np.float32)
    pk = np.zeros((N_EXTRA, N_HEADS, FOX_DIM), np.float32)
    oq = np.zeros((1, FOX_DIM), np.float32)
    ok = np.zeros((1, FOX_DIM), np.float32)
    for h in range(N_HEADS):
        base = LANES * (h // 2)
        for c in range(N_EXTRA):
            if h % 2 == 0:
                pq[c, h, base + c] = 1.0
                pk[c, h, base + N_EXTRA + c] = -1.0
                oq[0, base + N_EXTRA + c] = 1.0
                ok[0, base + c] = 1.0
            else:
                pq[c, h, base + 3 * N_EXTRA + c] = 1.0
                pk[c, h, base + 2 * N_EXTRA + c] = -1.0
                oq[0, base + 2 * N_EXTRA + c] = 1.0
                ok[0, base + 3 * N_EXTRA + c] = 1.0
    return (jnp.asarray(pq, BF16), jnp.asarray(pk, BF16), jnp.asarray(oq), jnp.asarray(ok))


def _fbias_kernel(logf_ref, pq_ref, pk_ref, oq_ref, ok_ref, kx_ref, qx_ref, carry_ref, *, blk, nblk):
    r = lax.broadcasted_iota(jnp.int32, (blk, blk), 0)
    c = lax.broadcasted_iota(jnp.int32, (blk, blk), 1)
    ltri = jnp.where(r >= c, 1.0, 0.0).astype(BF16)

    @pl.when(pl.program_id(1) == 0)
    def _():
        carry_ref[...] = jnp.zeros(carry_ref.shape, F32)

    def body(i, carry):
        off = pl.multiple_of(i * blk, blk)
        f = _sel_dot(ltri, logf_ref[0, pl.ds(off, blk), :]) + carry
        parts = _split3(f * LOG2E)
        kx = ok_ref[...]
        qx = oq_ref[...]
        for p in range(N_EXTRA):
            kx = kx + _dot(parts[p], pk_ref[p])
            qx = qx + _dot(parts[p], pq_ref[p])
        kx_ref[0, pl.ds(off, blk), :] = kx.astype(BF16)
        qx_ref[0, pl.ds(off, blk), :] = qx.astype(BF16)
        return f[blk - 1:blk, :]

    carry_ref[...] = lax.fori_loop(0, nblk, body, carry_ref[...])


def _fbias(logf3, blk):
    b, t, _ = logf3.shape
    tt = t if t <= 4096 else 2048
    assert t % tt == 0 and tt % blk == 0
    pq, pk, oq, ok = _placement_constants()
    out = jax.ShapeDtypeStruct((b, t, FOX_DIM), BF16)
    tile = lambda w: pl.BlockSpec((1, tt, w), lambda bi, ti: (bi, ti, 0))
    return pl.pallas_call(
        functools.partial(_fbias_kernel, blk=blk, nblk=tt // blk),
        grid=(b, t // tt),
        in_specs=[tile(N_HEADS), _resident((N_EXTRA, N_HEADS, FOX_DIM)), _resident((N_EXTRA, N_HEADS, FOX_DIM)),
                  _resident((1, FOX_DIM)), _resident((1, FOX_DIM))],
        out_specs=(tile(FOX_DIM), tile(FOX_DIM)),
        out_shape=(out, out),
        scratch_shapes=[pltpu.VMEM((1, N_HEADS), F32)],
        compiler_params=_params(("arbitrary", "arbitrary")),
        name="fbias",
    )(logf3, pq, pk, oq, ok)


def _attn_kernel(q_ref, qx_ref, k_ref, kx_ref, vt_ref, o_ref, kcat_ref, qs_ref, acc_ref, *, tq, tsub, tk, past):
    i = pl.program_id(2)
    nsub = tq // tsub
    chains = [(sb, a) for sb in range(nsub) for a in range(2)]

    @pl.when(i == 0)
    def _():
        kcat_ref[:, 0:LANES] = k_ref[0]
        kcat_ref[:, LANES:2 * LANES] = kx_ref[0]

    lane = lax.broadcasted_iota(jnp.int32, (tsub, LANES), 1)
    for sb in range(nsub):
        q2 = q_ref[0, tsub * sb:tsub * (sb + 1), :]
        qx2 = qx_ref[0, tsub * sb:tsub * (sb + 1), :]
        zero = jnp.zeros_like(q2)
        qs_ref[2 * sb, :, 0:LANES] = jnp.where(lane < HEAD_DIM, q2, zero)
        qs_ref[2 * sb, :, LANES:2 * LANES] = jnp.where(lane < Q_EXTRA_SPLIT, qx2, zero)
        qs_ref[2 * sb + 1, :, 0:LANES] = jnp.where(lane >= HEAD_DIM, q2, zero)
        qs_ref[2 * sb + 1, :, LANES:2 * LANES] = jnp.where(lane >= Q_EXTRA_SPLIT, qx2, zero)
    acc_ref[...] = jnp.zeros(acc_ref.shape, F32)

    q_lo = past + i * tq
    n_full = (q_lo + 1) // tk
    n_total = (q_lo + tq - 1) // tk + 1

    def step(j, carry, masked):
        off = pl.multiple_of(j * tk, tk)
        kc = kcat_ref[pl.ds(off, tk), :]
        vts = [vt_ref[0, HEAD_DIM * a:HEAD_DIM * (a + 1), pl.ds(off, tk)] for a in range(2)]
        scores = []
        for c, (sb, a) in enumerate(chains):
            s = _dot_nt(kc, qs_ref[c])
            if masked:
                kpos = off + lax.broadcasted_iota(jnp.int32, (tk, tsub), 0)
                qpos = q_lo + tsub * sb + lax.broadcasted_iota(jnp.int32, (tk, tsub), 1)
                s = jnp.where(kpos <= qpos, s, NEG_BIG)
            scores.append(s)
        out, probs = [], []
        for c, (sb, a) in enumerate(chains):
            m_prev, l_prev = carry[c]
            m_new = jnp.maximum(m_prev, jnp.max(scores[c], axis=0, keepdims=True))
            alpha = jnp.exp2(m_prev - m_new)
            p = jnp.exp2(scores[c] - m_new)
            out.append((m_new, alpha * l_prev + jnp.sum(p, axis=0, keepdims=True)))
            probs.append((alpha, p.astype(BF16)))
        for c, (sb, a) in enumerate(chains):
            alpha, p = probs[c]
            acc_ref[c] = alpha * acc_ref[c] + _dot(vts[a], p)
        return tuple(out)

    init = tuple((jnp.full((1, tsub), NEG_BIG, F32), jnp.zeros((1, tsub), F32)) for _ in chains)
    carry = lax.fori_loop(0, n_full, functools.partial(step, masked=False), init)
    carry = lax.fori_loop(n_full, n_total, functools.partial(step, masked=True), carry)

    for sb in range(nsub):
        ot = jnp.concatenate([acc_ref[2 * sb + a] * (1.0 / carry[2 * sb + a][1]) for a in range(2)], axis=0)
        o_ref[0, tsub * sb:tsub * (sb + 1), :] = ot.T.astype(BF16)


def _attention(q, qx, k, kx, vt, *, past, tq, tk):
    b, t, _ = q.shape
    tkeys = k.shape[1]
    npairs = N_HEADS // 2
    assert t % tq == 0 and tkeys % tk == 0 and past % tq == 0
    tsub = min(tq, MXU_DIM)
    assert tq % tsub == 0
    nchain = 2 * (tq // tsub)
    qoff = past // tq
    keys = pl.BlockSpec((1, tkeys, LANES), lambda bi, p, i: (bi, 0, p))
    return pl.pallas_call(
        functools.partial(_attn_kernel, tq=tq, tsub=tsub, tk=tk, past=past),
        grid=(b, npairs, t // tq),
        in_specs=[pl.BlockSpec((1, tq, LANES), lambda bi, p, i: (bi, i, p)),
                  pl.BlockSpec((1, tq, LANES), lambda bi, p, i: (bi, qoff + i, p)),
                  keys, keys,
                  pl.BlockSpec((1, LANES, tkeys), lambda bi, p, i: (bi, p, 0))],
        out_specs=pl.BlockSpec((1, tq, LANES), lambda bi, p, i: (bi, i, p)),
        out_shape=jax.ShapeDtypeStruct((b, t, FOX_DIM), BF16),
        scratch_shapes=[pltpu.VMEM((tkeys, 2 * LANES), BF16),
                        pltpu.VMEM((nchain, tsub, 2 * LANES), BF16),
                        pltpu.VMEM((nchain, HEAD_DIM, tsub), F32)],
        compiler_params=_params(("arbitrary", "arbitrary", "arbitrary")),
        name="fox_attention",
    )(q, qx, k, kx, vt)


def _ssd_kernel(xbc_ref, z_ref, dt_ref, cprev_ref, sprev_ref, wconv_ref, bconv_ref, alog_ref,
                dskip_ref, gssd_ref, rexp_ref, rexpt_ref,
                y_ref, hfin_ref, ext_ref, state_ref, *, L):
    c = pl.program_id(1)

    @pl.when(c == 0)
    def _():
        state_ref[...] = sprev_ref[0]
        ext_ref[0:SUBLANES, :] = cprev_ref[0]

    x_tile = xbc_ref[0]
    ext_ref[SUBLANES:SUBLANES + L, :] = x_tile
    w = wconv_ref[...]
    pre = bconv_ref[...]
    for j in range(CONV_W):
        start = SUBLANES - (CONV_W - 1) + j
        pre = pre + ext_ref[start:start + L, :] * w[j:j + 1, :]
    ext_ref[0:SUBLANES, :] = x_tile[L - SUBLANES:L, :]
    u = pre * _sigmoid(pre)
    xs = u[:, 0:SSD_DIM]
    bmat = u[:, SSD_DIM:SSD_DIM + SSD_GROUPS * SSD_STATE].astype(BF16)
    cmat = u[:, SSD_DIM + SSD_GROUPS * SSD_STATE:].astype(BF16)

    dt = dt_ref[0]
    a = dt * (-jnp.exp(alog_ref[...]))
    r = lax.broadcasted_iota(jnp.int32, (L, L), 0)
    cc = lax.broadcasted_iota(jnp.int32, (L, L), 1)
    causal = r >= cc
    ltri = jnp.where(causal, 1.0, 0.0).astype(BF16)
    cum = _sel_dot(ltri, a)
    e_r = lax.broadcasted_iota(jnp.int32, (N_HEADS, N_HEADS), 0)
    e_c = lax.broadcasted_iota(jnp.int32, (N_HEADS, N_HEADS), 1)
    eye = jnp.where(e_r == e_c, 1.0, 0.0).astype(BF16)
    cum_t = _sel_dot_nt(eye, cum)
    last = cum[L - 1:L, :]

    rexp = rexp_ref[...]
    dtx = _dot_sel(dt, rexp)
    wendx = _dot_sel(jnp.exp(last - cum), rexp)
    ecumx = _dot_sel(jnp.exp(cum), rexp)
    xdt = xs * dtx
    xdt_b = xdt.astype(BF16)
    xw_b = (xdt * wendx).astype(BF16)
    elast = jnp.exp(cum_t[:, L - 1:L])
    elast_b = jnp.broadcast_to(elast, (N_HEADS, SSD_STATE))
    rexpt = rexpt_ref[...]
    lane = lax.broadcasted_iota(jnp.int32, (L, LANES), 1)

    heads_per_group = N_HEADS // SSD_GROUPS
    ys = []
    for g in range(SSD_GROUPS):
        gs = slice(GROUP_DIM * g, GROUP_DIM * (g + 1))
        cg = cmat[:, SSD_STATE * g:SSD_STATE * (g + 1)]
        bg = bmat[:, SSD_STATE * g:SSD_STATE * (g + 1)]
        cb = _dot_nt(cg, bg)
        st = state_ref[gs, :]
        yoff = _dot_nt(cg, st.astype(BF16))
        for pr in range(heads_per_group // 2):
            ps = slice(GROUP_DIM * g + LANES * pr, GROUP_DIM * g + LANES * (pr + 1))
            slab = xdt_b[:, ps]
            res = []
            for hh in range(2):
                hd = heads_per_group * g + 2 * pr + hh
                seg = cum[:, hd:hd + 1] - cum_t[hd:hd + 1, :]
                dec = jnp.exp(jnp.where(causal, seg, -jnp.inf))
                res.append(_dot((cb * dec).astype(BF16), slab))
            ydiag = jnp.where(lane < HEAD_DIM, res[0], res[1])
            ys.append(ydiag + yoff[:, LANES * pr:LANES * (pr + 1)] * ecumx[:, ps])
        snew = _dot_tn(xw_b[:, gs], bg)
        scale = _sel_dot(rexpt[gs, :], elast_b)
        state_ref[gs, :] = scale * st + snew

    y = jnp.concatenate(ys, axis=-1) + dskip_ref[...] * xs
    zt = z_ref[0]
    ug = y * (zt * _sigmoid(zt))
    outs = []
    for g in range(SSD_GROUPS):
        ugg = ug[:, GROUP_DIM * g:GROUP_DIM * (g + 1)]
        outs.append(ugg * lax.rsqrt(jnp.mean(ugg * ugg, axis=-1, keepdims=True) + EPS))
    y_ref[0] = (jnp.concatenate(outs, axis=-1) * gssd_ref[...]).astype(BF16)

    @pl.when(c == pl.num_programs(1) - 1)
    def _():
        hfin_ref[0] = state_ref[...]


def _ssd(xbc, z, dt, conv_prev8, ssm_prev, w_conv, b_conv, a_log, d_skip, g_ssd, *, L):
    b, t, _ = xbc.shape
    assert t % L == 0 and L % SUBLANES == 0
    head_of_lane = jnp.arange(SSD_DIM) // HEAD_DIM
    rexp = (jnp.arange(N_HEADS)[:, None] == head_of_lane[None, :]).astype(BF16)
    tile = lambda w: pl.BlockSpec((1, L, w), lambda bi, ci: (bi, ci, 0))
    per_b = lambda s: pl.BlockSpec((1,) + s, lambda bi, ci: (bi, 0, 0))
    return pl.pallas_call(
        functools.partial(_ssd_kernel, L=L),
        grid=(b, t // L),
        in_specs=[tile(CONV_DIM), tile(SSD_DIM), tile(N_HEADS),
                  per_b((SUBLANES, CONV_DIM)), per_b((SSD_DIM, SSD_STATE)),
                  _resident((CONV_W, CONV_DIM)), _resident((1, CONV_DIM)), _resident((1, N_HEADS)),
                  _resident((1, SSD_DIM)), _resident((1, SSD_DIM)),
                  _resident((N_HEADS, SSD_DIM)), _resident((SSD_DIM, N_HEADS))],
        out_specs=(tile(SSD_DIM), per_b((SSD_DIM, SSD_STATE))),
        out_shape=(jax.ShapeDtypeStruct((b, t, SSD_DIM), BF16),
                   jax.ShapeDtypeStruct((b, SSD_DIM, SSD_STATE), F32)),
        scratch_shapes=[pltpu.VMEM((L + SUBLANES, CONV_DIM), F32),
                        pltpu.VMEM((SSD_DIM, SSD_STATE), F32)],
        compiler_params=_params(("arbitrary", "arbitrary")),
        name="ssd_scan",
    )(xbc, z, dt, conv_prev8, ssm_prev, w_conv, b_conv.reshape(1, CONV_DIM), a_log.reshape(1, N_HEADS),
      jnp.repeat(d_skip, HEAD_DIM).reshape(1, SSD_DIM), g_ssd.reshape(1, SSD_DIM), rexp, rexp.T)


def _outproj_kernel(x_ref, attn_ref, y_ref, wo_ref, o_ref):
    mix = _dot(attn_ref[...], wo_ref[0:FOX_DIM, :]) + _dot(y_ref[...], wo_ref[FOX_DIM:, :])
    o_ref[...] = x_ref[...] + mix


def _outproj(x2d, attn, y, w_out):
    n, d = x2d.shape
    tm = min(512, n)
    assert n % tm == 0
    row = lambda w: pl.BlockSpec((tm, w), lambda i: (i, 0))
    return pl.pallas_call(
        _outproj_kernel,
        grid=(n // tm,),
        in_specs=[row(d), row(FOX_DIM), row(SSD_DIM), _resident((FOX_DIM + SSD_DIM, d))],
        out_specs=row(d),
        out_shape=jax.ShapeDtypeStruct((n, d), F32),
        compiler_params=_params(("arbitrary",)),
        name="outproj",
    )(x2d, attn, y, w_out)


def _ffn_kernel(x_ref, g_ref, wg_ref, wu_ref, wd_ref, o_ref, hf_ref):
    j = pl.program_id(1)

    @pl.when(j == 0)
    def _():
        x = x_ref[...]
        ms = jnp.mean(x * x, axis=-1, keepdims=True)
        hf_ref[...] = (x * lax.rsqrt(ms + EPS) * g_ref[...]).astype(BF16)
        o_ref[...] = x

    hf = hf_ref[...]
    gate = _dot(hf, wg_ref[...])
    act = (gate * _sigmoid(gate) * _dot(hf, wu_ref[...])).astype(BF16)
    o_ref[...] += _dot(act, wd_ref[...])


def _ffn(x2d, g_ffn, w_gate, w_up, w_down):
    n, d = x2d.shape
    dff = w_gate.shape[1]
    tm, tf = min(512, n), 512
    assert n % tm == 0 and dff % tf == 0
    return pl.pallas_call(
        _ffn_kernel,
        grid=(n // tm, dff // tf),
        in_specs=[pl.BlockSpec((tm, d), lambda i, j: (i, 0)),
                  _resident((1, d)),
                  pl.BlockSpec((d, tf), lambda i, j: (0, j)),
                  pl.BlockSpec((d, tf), lambda i, j: (0, j)),
                  pl.BlockSpec((tf, d), lambda i, j: (j, 0))],
        out_specs=pl.BlockSpec((tm, d), lambda i, j: (i, 0)),
        out_shape=jax.ShapeDtypeStruct((n, d), F32),
        scratch_shapes=[pltpu.VMEM((tm, d), BF16)],
        compiler_params=_params(("arbitrary", "arbitrary")),
        name="ffn",
    )(x2d, g_ffn.reshape(1, d), w_gate, w_up, w_down)


def _pack_w_in(w_in):
    d = w_in.shape[0]
    o_f = 3 * FOX_DIM
    o_z = o_f + N_HEADS
    o_xbc = o_z + SSD_DIM
    o_dt = o_xbc + CONV_DIM
    pad = jnp.zeros((d, LANES - N_HEADS), w_in.dtype)
    return jnp.concatenate([w_in[:, :o_f], w_in[:, o_z:o_xbc], w_in[:, o_xbc:o_dt],
                            w_in[:, o_f:o_z], pad, w_in[:, o_dt:], pad], axis=1).astype(BF16)


def _layer(x, conv_prev, ssm_prev, k_prev, v_prev, logf_prev, wts, *, attn_tq, attn_tk, fbias_blk, ssd_chunk):
    (g_mix, w_all, f_bias, g_q, g_k, w_conv, b_conv, dt_bias, a_log, d_skip, g_ssd,
     w_out, g_ffn, w_gate, w_up, w_down) = wts
    b, t, d = x.shape
    n = b * t
    x2d = x.reshape(n, d)
    qb, k, kb, v, vb, z, xbc, logf, dt = _inproj(x2d, g_mix, w_all, g_q, g_k, f_bias, dt_bias)

    past = 0 if k_prev is None else k_prev.shape[1]
    kb3, vb3, logf3 = kb.reshape(b, t, FOX_DIM), vb.reshape(b, t, FOX_DIM), logf.reshape(b, t, N_HEADS)
    logf_all = logf3
    if past:
        kb3 = jnp.concatenate([k_prev.reshape(b, past, FOX_DIM).astype(BF16), kb3], axis=1)
        vb3 = jnp.concatenate([v_prev.reshape(b, past, FOX_DIM).astype(BF16), vb3], axis=1)
        logf_all = jnp.concatenate([logf_prev.astype(F32), logf3], axis=1)
    tkeys = past + t
    tk = attn_tk
    if tkeys % tk:
        tk = -(-tkeys // LANES) * LANES
        grow = ((0, 0), (0, tk - tkeys), (0, 0))
        kb3, vb3, logf_all = jnp.pad(kb3, grow), jnp.pad(vb3, grow), jnp.pad(logf_all, grow)
    kx, qx = _fbias(logf_all, min(fbias_blk, tk))
    attn = _attention(qb.reshape(b, t, FOX_DIM), qx, kb3, kx, jnp.transpose(vb3, (0, 2, 1)),
                      past=past, tq=attn_tq, tk=tk)

    conv_prev8 = jnp.pad(conv_prev.astype(F32), ((0, 0), (SUBLANES - (CONV_W - 1), 0), (0, 0)))
    y, h_last = _ssd(xbc.reshape(b, t, CONV_DIM), z.reshape(b, t, SSD_DIM), dt.reshape(b, t, N_HEADS),
                     conv_prev8, ssm_prev.astype(F32).reshape(b, SSD_DIM, SSD_STATE),
                     w_conv, b_conv, a_log, d_skip, g_ssd, L=ssd_chunk)

    x1 = _outproj(x2d, attn.reshape(n, FOX_DIM), y.reshape(n, SSD_DIM), w_out)
    out = _ffn(x1, g_ffn, w_gate, w_up, w_down)

    xbc_all = jnp.concatenate([conv_prev.astype(F32), xbc.reshape(b, t, CONV_DIM)], axis=1)
    return (out.reshape(b, t, d), xbc_all[:, -(CONV_W - 1):],
            h_last.reshape(b, N_HEADS, HEAD_DIM, SSD_STATE),
            k.reshape(b, t, N_HEADS, HEAD_DIM), v.reshape(b, t, N_HEADS, HEAD_DIM), logf3)


def kernel(x_prompt, x_sample, cache_conv, state_ssm, cache_fox_k, cache_fox_v, cache_fox_logf, g_mix, w_in, f_bias, g_q, g_k, w_conv, b_conv, dt_bias, a_log, d_skip, g_ssd, w_out, g_ffn, w_gate, w_up, w_down):
    depth = g_mix.shape[0]
    yp, ys = x_prompt, x_sample
    bp = x_prompt.shape[0]
    outs_p = [[] for _ in range(5)]
    outs_s = [[] for _ in range(5)]
    for i in range(depth):
        wts = (g_mix[i], _pack_w_in(w_in[i]), f_bias[i], g_q[i], g_k[i], w_conv[i], b_conv[i], dt_bias[i],
               a_log[i], d_skip[i], g_ssd[i], w_out[i].astype(BF16), g_ffn[i],
               w_gate[i].astype(BF16), w_up[i].astype(BF16), w_down[i].astype(BF16))
        tp = yp.shape[1]
        yp, *rest = _layer(
            yp, jnp.zeros((bp, CONV_W - 1, CONV_DIM), F32),
            jnp.zeros((bp, N_HEADS, HEAD_DIM, SSD_STATE), F32), None, None, None, wts,
            attn_tq=min(512, tp), attn_tk=min(1024, tp), fbias_blk=256, ssd_chunk=min(128, tp))
        for lst, val in zip(outs_p, rest):
            lst.append(val)
        tsamp = ys.shape[1]
        ys, *rest = _layer(
            ys, cache_conv[i], state_ssm[i], cache_fox_k[i], cache_fox_v[i], cache_fox_logf[i], wts,
            attn_tq=tsamp, attn_tk=LANES, fbias_blk=LANES, ssd_chunk=tsamp)
        for lst, val in zip(outs_s, rest):
            lst.append(val)
    return (yp, ys, *[jnp.stack(l) for l in outs_p], *[jnp.stack(l) for l in outs_s])
```

```python
import functools

import jax
import jax.numpy as jnp
import numpy as np
from jax import lax
from jax.experimental import pallas as pl
from jax.experimental.pallas import tpu as pltpu

F32 = jnp.float32
BF16 = jnp.bfloat16

EPS = 1e-6
HEAD_DIM = 64
N_HEADS = 16
FOX_DIM = 1024
SSD_DIM = 1024
SSD_GROUPS = 2
SSD_STATE = 128
GROUP_DIM = SSD_DIM // SSD_GROUPS
CONV_W = 4
CONV_DIM = SSD_DIM + 2 * SSD_GROUPS * SSD_STATE
LANES = 128
SUBLANES = 8
MXU_DIM = 256
NEG_BIG = -1e30
LOG2E = 1.4426950408889634
VMEM_LIMIT = 56 * 1024 * 1024

COL_Q, COL_K, COL_V, COL_Z = 0, FOX_DIM, 2 * FOX_DIM, 3 * FOX_DIM
COL_XBC = COL_Z + SSD_DIM
COL_F = COL_XBC + CONV_DIM
COL_DT = COL_F + LANES
IN_COLS = COL_DT + LANES


def _dot(a, b):
    return jnp.dot(a, b, preferred_element_type=F32)


def _dot_nt(a, b):
    return lax.dot_general(a, b, (((1,), (1,)), ((), ())), preferred_element_type=F32)


def _dot_tn(a, b):
    return lax.dot_general(a, b, (((0,), (0,)), ((), ())), preferred_element_type=F32)


def _split3(x):
    x1 = x.astype(BF16)
    r = x - x1.astype(F32)
    x2 = r.astype(BF16)
    r = r - x2.astype(F32)
    return x1, x2, r.astype(BF16)


def _sel_dot(sel, x):
    x1, x2, x3 = _split3(x)
    return _dot(sel, x1) + _dot(sel, x2) + _dot(sel, x3)


def _dot_sel(x, sel):
    x1, x2, x3 = _split3(x)
    return _dot(x1, sel) + _dot(x2, sel) + _dot(x3, sel)


def _sel_dot_nt(sel, x):
    x1, x2, x3 = _split3(x)
    return _dot_nt(sel, x1) + _dot_nt(sel, x2) + _dot_nt(sel, x3)


def _sigmoid(x):
    return 1.0 / (1.0 + jnp.exp(-x))


def _softplus(x):
    return jnp.maximum(x, 0.0) + jnp.log1p(jnp.exp(-jnp.abs(x)))


def _resident(shape):
    nd = len(shape)
    return pl.BlockSpec(shape, lambda *_: (0,) * nd, pipeline_mode=pl.Buffered(1))


def _params(sem):
    return pltpu.CompilerParams(dimension_semantics=sem, vmem_limit_bytes=VMEM_LIMIT)


def _inproj_kernel(x_ref, gmix_ref, w_ref, e_ref, gq_ref, gk_ref, fb_ref, dtb_ref,
                   q_ref, k_ref, kb_ref, v_ref, vb_ref, z_ref, xbc_ref, logf_ref, dt_ref):
    x = x_ref[...]
    ms = jnp.mean(x * x, axis=-1, keepdims=True)
    h = (x * lax.rsqrt(ms + EPS) * gmix_ref[...]).astype(BF16)
    e = e_ref[...]

    def head_rms(y, g):
        ysq = y * y
        hi = ysq.astype(BF16)
        lo = (ysq - hi.astype(F32)).astype(BF16)
        parts = []
        for c in range(FOX_DIM // MXU_DIM):
            sl = slice(MXU_DIM * c, MXU_DIM * (c + 1))
            parts.append(_dot(hi[:, sl], e) + _dot(lo[:, sl], e))
        ss = jnp.concatenate(parts, axis=-1)
        return y * lax.rsqrt(ss * (1.0 / HEAD_DIM) + EPS) * g

    q = _dot(h, w_ref[:, COL_Q:COL_Q + FOX_DIM])
    q_ref[...] = (head_rms(q, gq_ref[...]) * (HEAD_DIM ** -0.5 * LOG2E)).astype(BF16)
    k = head_rms(_dot(h, w_ref[:, COL_K:COL_K + FOX_DIM]), gk_ref[...])
    k_ref[...] = k
    kb_ref[...] = k.astype(BF16)
    v = _dot(h, w_ref[:, COL_V:COL_V + FOX_DIM])
    v_ref[...] = v
    vb_ref[...] = v.astype(BF16)
    z_ref[...] = _dot(h, w_ref[:, COL_Z:COL_Z + SSD_DIM])
    xbc_ref[...] = _dot(h, w_ref[:, COL_XBC:COL_XBC + CONV_DIM])
    f_raw = _dot(h, w_ref[:, COL_F:COL_F + LANES])[:, :N_HEADS]
    logf_ref[...] = -_softplus(-(f_raw + fb_ref[...]))
    dt_raw = _dot(h, w_ref[:, COL_DT:COL_DT + LANES])[:, :N_HEADS]
    dt_ref[...] = _softplus(dt_raw + dtb_ref[...])


def _inproj(x2d, g_mix, w_all, g_q, g_k, f_bias, dt_bias):
    n, d = x2d.shape
    tm = min(256, n)
    assert n % tm == 0
    blk = jnp.arange(MXU_DIM) // HEAD_DIM
    e = (blk[:, None] == blk[None, :]).astype(BF16)
    row = lambda w: pl.BlockSpec((tm, w), lambda i: (i, 0))
    out_shapes = (
        jax.ShapeDtypeStruct((n, FOX_DIM), BF16),
        jax.ShapeDtypeStruct((n, FOX_DIM), F32),
        jax.ShapeDtypeStruct((n, FOX_DIM), BF16),
        jax.ShapeDtypeStruct((n, FOX_DIM), F32),
        jax.ShapeDtypeStruct((n, FOX_DIM), BF16),
        jax.ShapeDtypeStruct((n, SSD_DIM), F32),
        jax.ShapeDtypeStruct((n, CONV_DIM), F32),
        jax.ShapeDtypeStruct((n, N_HEADS), F32),
        jax.ShapeDtypeStruct((n, N_HEADS), F32),
    )
    return pl.pallas_call(
        _inproj_kernel,
        grid=(n // tm,),
        in_specs=[row(d), _resident((1, d)), _resident((d, IN_COLS)), _resident((MXU_DIM, MXU_DIM)),
                  _resident((1, FOX_DIM)), _resident((1, FOX_DIM)),
                  _resident((1, N_HEADS)), _resident((1, N_HEADS))],
        out_specs=(row(FOX_DIM), row(FOX_DIM), row(FOX_DIM), row(FOX_DIM), row(FOX_DIM),
                   row(SSD_DIM), row(CONV_DIM), row(N_HEADS), row(N_HEADS)),
        out_shape=out_shapes,
        compiler_params=_params(("arbitrary",)),
        name="inproj",
    )(x2d, g_mix.reshape(1, d), w_all, e,
      jnp.tile(g_q, N_HEADS).reshape(1, FOX_DIM), jnp.tile(g_k, N_HEADS).reshape(1, FOX_DIM),
      f_bias.reshape(1, N_HEADS), dt_bias.reshape(1, N_HEADS))


N_EXTRA = 3
Q_EXTRA_SPLIT = 2 * N_EXTRA


def _placement_constants():
    pq = np.zeros((N_EXTRA, N_HEADS, FOX_DIM), np.float32)
    pk = np.zeros((N_EXTRA, N_HEADS, FOX_DIM), np.float32)
    oq = np.zeros((1, FOX_DIM), np.float32)
    ok = np.zeros((1, FOX_DIM), np.float32)
    for h in range(N_HEADS):
        base = LANES * (h // 2)
        for c in range(N_EXTRA):
            if h % 2 == 0:
                pq[c, h, base + c] = 1.0
                pk[c, h, base + N_EXTRA + c] = -1.0
                oq[0, base + N_EXTRA + c] = 1.0
                ok[0, base + c] = 1.0
            else:
                pq[c, h, base + 3 * N_EXTRA + c] = 1.0
                pk[c, h, base + 2 * N_EXTRA + c] = -1.0
                oq[0, base + 2 * N_EXTRA + c] = 1.0
                ok[0, base + 3 * N_EXTRA + c] = 1.0
    return (jnp.asarray(pq, BF16), jnp.asarray(pk, BF16), jnp.asarray(oq), jnp.asarray(ok))


def _fbias_kernel(logf_ref, pq_ref, pk_ref, oq_ref, ok_ref, kx_ref, qx_ref, carry_ref, *, blk, nblk):
    r = lax.broadcasted_iota(jnp.int32, (blk, blk), 0)
    c = lax.broadcasted_iota(jnp.int32, (blk, blk), 1)
    ltri = jnp.where(r >= c, 1.0, 0.0).astype(BF16)

    @pl.when(pl.program_id(1) == 0)
    def _():
        carry_ref[...] = jnp.zeros(carry_ref.shape, F32)

    def body(i, carry):
        off = pl.multiple_of(i * blk, blk)
        f = _sel_dot(ltri, logf_ref[0, pl.ds(off, blk), :]) + carry
        parts = _split3(f * LOG2E)
        kx = ok_ref[...]
        qx = oq_ref[...]
        for p in range(N_EXTRA):
            kx = kx + _dot(parts[p], pk_ref[p])
            qx = qx + _dot(parts[p], pq_ref[p])
        kx_ref[0, pl.ds(off, blk), :] = kx.astype(BF16)
        qx_ref[0, pl.ds(off, blk), :] = qx.astype(BF16)
        return f[blk - 1:blk, :]

    carry_ref[...] = lax.fori_loop(0, nblk, body, carry_ref[...])


def _fbias(logf3, blk):
    b, t, _ = logf3.shape
    tt = t if t <= 4096 else 2048
    assert t % tt == 0 and tt % blk == 0
    pq, pk, oq, ok = _placement_constants()
    out = jax.ShapeDtypeStruct((b, t, FOX_DIM), BF16)
    tile = lambda w: pl.BlockSpec((1, tt, w), lambda bi, ti: (bi, ti, 0))
    return pl.pallas_call(
        functools.partial(_fbias_kernel, blk=blk, nblk=tt // blk),
        grid=(b, t // tt),
        in_specs=[tile(N_HEADS), _resident((N_EXTRA, N_HEADS, FOX_DIM)), _resident((N_EXTRA, N_HEADS, FOX_DIM)),
                  _resident((1, FOX_DIM)), _resident((1, FOX_DIM))],
        out_specs=(tile(FOX_DIM), tile(FOX_DIM)),
        out_shape=(out, out),
        scratch_shapes=[pltpu.VMEM((1, N_HEADS), F32)],
        compiler_params=_params(("arbitrary", "arbitrary")),
        name="fbias",
    )(logf3, pq, pk, oq, ok)


def _attn_kernel(q_ref, qx_ref, k_ref, kx_ref, vt_ref, o_ref, kcat_ref, qs_ref, acc_ref, sa_ref, sb_ref,
                 *, tq, tsub, tk, past):
    i = pl.program_id(2)
    nsub = tq // tsub
    chains = [(sb, a) for sb in range(nsub) for a in range(2)]

    @pl.when(i == 0)
    def _():
        kcat_ref[:, 0:LANES] = k_ref[0]
        kcat_ref[:, LANES:2 * LANES] = kx_ref[0]

    lane = lax.broadcasted_iota(jnp.int32, (tsub, LANES), 1)
    for sb in range(nsub):
        q2 = q_ref[0, tsub * sb:tsub * (sb + 1), :]
        qx2 = qx_ref[0, tsub * sb:tsub * (sb + 1), :]
        zero = jnp.zeros_like(q2)
        qs_ref[2 * sb, :, 0:LANES] = jnp.where(lane < HEAD_DIM, q2, zero)
        qs_ref[2 * sb, :, LANES:2 * LANES] = jnp.where(lane < Q_EXTRA_SPLIT, qx2, zero)
        qs_ref[2 * sb + 1, :, 0:LANES] = jnp.where(lane >= HEAD_DIM, q2, zero)
        qs_ref[2 * sb + 1, :, LANES:2 * LANES] = jnp.where(lane >= Q_EXTRA_SPLIT, qx2, zero)
    acc_ref[...] = jnp.zeros(acc_ref.shape, F32)

    q_lo = past + i * tq
    jm = q_lo // tk

    def qk_stage(j, s_ref, masked):
        off = pl.multiple_of(j * tk, tk)
        kc = kcat_ref[pl.ds(off, tk), :]
        mx = []
        for c, (sb, a) in enumerate(chains):
            s = _dot_nt(kc, qs_ref[c])
            if masked:
                kpos = off + lax.broadcasted_iota(jnp.int32, (tk, tsub), 0)
                qpos = q_lo + tsub * sb + lax.broadcasted_iota(jnp.int32, (tk, tsub), 1)
                s = jnp.where(kpos <= qpos, s, NEG_BIG)
            s_ref[c] = s
            mx.append(jnp.max(s, axis=0, keepdims=True))
        return tuple(mx)

    def sm_stage(j, s_ref, mx, state):
        off = pl.multiple_of(j * tk, tk)
        vts = [vt_ref[0, HEAD_DIM * a:HEAD_DIM * (a + 1), pl.ds(off, tk)] for a in range(2)]
        out, probs = [], []
        for c, (sb, a) in enumerate(chains):
            m_prev, l_prev = state[c]
            m_new = jnp.maximum(m_prev, mx[c])
            alpha = jnp.exp2(m_prev - m_new)
            p = jnp.exp2(s_ref[c] - m_new)
            out.append((m_new, alpha * l_prev + jnp.sum(p, axis=0, keepdims=True)))
            probs.append((alpha, p.astype(BF16)))
        for c, (sb, a) in enumerate(chains):
            alpha, p = probs[c]
            acc_ref[c] = alpha * acc_ref[c] + _dot(vts[a], p)
        return tuple(out)

    init = tuple((jnp.full((1, tsub), NEG_BIG, F32), jnp.zeros((1, tsub), F32)) for _ in chains)
    mx0 = qk_stage(jm, sa_ref, True)

    def pair_body(u, carry):
        jprev, mx_a, state = carry
        mx_b = qk_stage(2 * u, sb_ref, False)
        state = sm_stage(jprev, sa_ref, mx_a, state)
        mx_a = qk_stage(2 * u + 1, sa_ref, False)
        state = sm_stage(2 * u, sb_ref, mx_b, state)
        return 2 * u + 1, mx_a, state

    jprev, mx_a, state = lax.fori_loop(0, jm // 2, pair_body, (jm, mx0, init))

    def odd_tail(args):
        jprev, mx_a, state = args
        mx_b = qk_stage(jm - 1, sb_ref, False)
        state = sm_stage(jprev, sa_ref, mx_a, state)
        return sm_stage(jm - 1, sb_ref, mx_b, state)

    def even_tail(args):
        jprev, mx_a, state = args
        return sm_stage(jprev, sa_ref, mx_a, state)

    state = lax.cond(jm % 2 == 1, odd_tail, even_tail, (jprev, mx_a, state))

    for sb in range(nsub):
        ot = jnp.concatenate([acc_ref[2 * sb + a] * (1.0 / state[2 * sb + a][1]) for a in range(2)], axis=0)
        o_ref[0, tsub * sb:tsub * (sb + 1), :] = ot.T.astype(BF16)


def _attention(q, qx, k, kx, vt, *, past, tq, tk):
    b, t, _ = q.shape
    tkeys = k.shape[1]
    npairs = N_HEADS // 2
    assert t % tq == 0 and tkeys % tk == 0 and past % tq == 0 and tk % tq == 0
    tsub = min(tq, MXU_DIM)
    assert tq % tsub == 0
    nchain = 2 * (tq // tsub)
    qoff = past // tq
    keys = pl.BlockSpec((1, tkeys, LANES), lambda bi, p, i: (bi, 0, p))
    return pl.pallas_call(
        functools.partial(_attn_kernel, tq=tq, tsub=tsub, tk=tk, past=past),
        grid=(b, npairs, t // tq),
        in_specs=[pl.BlockSpec((1, tq, LANES), lambda bi, p, i: (bi, i, p)),
                  pl.BlockSpec((1, tq, LANES), lambda bi, p, i: (bi, qoff + i, p)),
                  keys, keys,
                  pl.BlockSpec((1, LANES, tkeys), lambda bi, p, i: (bi, p, 0))],
        out_specs=pl.BlockSpec((1, tq, LANES), lambda bi, p, i: (bi, i, p)),
        out_shape=jax.ShapeDtypeStruct((b, t, FOX_DIM), BF16),
        scratch_shapes=[pltpu.VMEM((tkeys, 2 * LANES), BF16),
                        pltpu.VMEM((nchain, tsub, 2 * LANES), BF16),
                        pltpu.VMEM((nchain, HEAD_DIM, tsub), F32),
                        pltpu.VMEM((nchain, tk, tsub), F32),
                        pltpu.VMEM((nchain, tk, tsub), F32)],
        compiler_params=_params(("arbitrary", "arbitrary", "arbitrary")),
        name="fox_attention",
    )(q, qx, k, kx, vt)


def _ssd_kernel(xbc_ref, z_ref, dt_ref, cprev_ref, sprev_ref, wconv_ref, bconv_ref, alog_ref,
                dskip_ref, gssd_ref, rexp_ref, rexpt_ref,
                y_ref, hfin_ref, ext_ref, state_ref, *, L):
    c = pl.program_id(1)

    @pl.when(c == 0)
    def _():
        state_ref[...] = sprev_ref[0]
        ext_ref[0:SUBLANES, :] = cprev_ref[0]

    x_tile = xbc_ref[0]
    ext_ref[SUBLANES:SUBLANES + L, :] = x_tile
    w = wconv_ref[...]
    pre = bconv_ref[...]
    for j in range(CONV_W):
        start = SUBLANES - (CONV_W - 1) + j
        pre = pre + ext_ref[start:start + L, :] * w[j:j + 1, :]
    ext_ref[0:SUBLANES, :] = x_tile[L - SUBLANES:L, :]
    u = pre * _sigmoid(pre)
    xs = u[:, 0:SSD_DIM]
    bmat = u[:, SSD_DIM:SSD_DIM + SSD_GROUPS * SSD_STATE].astype(BF16)
    cmat = u[:, SSD_DIM + SSD_GROUPS * SSD_STATE:].astype(BF16)

    dt = dt_ref[0]
    a = dt * (-jnp.exp(alog_ref[...]))
    r = lax.broadcasted_iota(jnp.int32, (L, L), 0)
    cc = lax.broadcasted_iota(jnp.int32, (L, L), 1)
    causal = r >= cc
    ltri = jnp.where(causal, 1.0, 0.0).astype(BF16)
    cum = _sel_dot(ltri, a)
    e_r = lax.broadcasted_iota(jnp.int32, (N_HEADS, N_HEADS), 0)
    e_c = lax.broadcasted_iota(jnp.int32, (N_HEADS, N_HEADS), 1)
    eye = jnp.where(e_r == e_c, 1.0, 0.0).astype(BF16)
    cum_t = _sel_dot_nt(eye, cum)
    last = cum[L - 1:L, :]

    rexp = rexp_ref[...]
    dtx = _dot_sel(dt, rexp)
    wendx = _dot_sel(jnp.exp(last - cum), rexp)
    ecumx = _dot_sel(jnp.exp(cum), rexp)
    xdt = xs * dtx
    xdt_b = xdt.astype(BF16)
    xw_b = (xdt * wendx).astype(BF16)
    elast = jnp.exp(cum_t[:, L - 1:L])
    elast_b = jnp.broadcast_to(elast, (N_HEADS, SSD_STATE))
    rexpt = rexpt_ref[...]
    lane = lax.broadcasted_iota(jnp.int32, (L, LANES), 1)

    heads_per_group = N_HEADS // SSD_GROUPS
    ys = []
    for g in range(SSD_GROUPS):
        gs = slice(GROUP_DIM * g, GROUP_DIM * (g + 1))
        cg = cmat[:, SSD_STATE * g:SSD_STATE * (g + 1)]
        bg = bmat[:, SSD_STATE * g:SSD_STATE * (g + 1)]
        cb = _dot_nt(cg, bg)
        st = state_ref[gs, :]
        yoff = _dot_nt(cg, st.astype(BF16))
        for pr in range(heads_per_group // 2):
            ps = slice(GROUP_DIM * g + LANES * pr, GROUP_DIM * g + LANES * (pr + 1))
            slab = xdt_b[:, ps]
            res = []
            for hh in range(2):
                hd = heads_per_group * g + 2 * pr + hh
                seg = cum[:, hd:hd + 1] - cum_t[hd:hd + 1, :]
                dec = jnp.exp(jnp.where(causal, seg, -jnp.inf))
                res.append(_dot((cb * dec).astype(BF16), slab))
            ydiag = jnp.where(lane < HEAD_DIM, res[0], res[1])
            ys.append(ydiag + yoff[:, LANES * pr:LANES * (pr + 1)] * ecumx[:, ps])
        snew = _dot_tn(xw_b[:, gs], bg)
        scale = _sel_dot(rexpt[gs, :], elast_b)
        state_ref[gs, :] = scale * st + snew

    y = jnp.concatenate(ys, axis=-1) + dskip_ref[...] * xs
    zt = z_ref[0]
    ug = y * (zt * _sigmoid(zt))
    outs = []
    for g in range(SSD_GROUPS):
        ugg = ug[:, GROUP_DIM * g:GROUP_DIM * (g + 1)]
        outs.append(ugg * lax.rsqrt(jnp.mean(ugg * ugg, axis=-1, keepdims=True) + EPS))
    y_ref[0] = (jnp.concatenate(outs, axis=-1) * gssd_ref[...]).astype(BF16)

    @pl.when(c == pl.num_programs(1) - 1)
    def _():
        hfin_ref[0] = state_ref[...]


def _ssd(xbc, z, dt, conv_prev8, ssm_prev, w_conv, b_conv, a_log, d_skip, g_ssd, *, L):
    b, t, _ = xbc.shape
    assert t % L == 0 and L % SUBLANES == 0
    head_of_lane = jnp.arange(SSD_DIM) // HEAD_DIM
    rexp = (jnp.arange(N_HEADS)[:, None] == head_of_lane[None, :]).astype(BF16)
    tile = lambda w: pl.BlockSpec((1, L, w), lambda bi, ci: (bi, ci, 0))
    per_b = lambda s: pl.BlockSpec((1,) + s, lambda bi, ci: (bi, 0, 0))
    return pl.pallas_call(
        functools.partial(_ssd_kernel, L=L),
        grid=(b, t // L),
        in_specs=[tile(CONV_DIM), tile(SSD_DIM), tile(N_HEADS),
                  per_b((SUBLANES, CONV_DIM)), per_b((SSD_DIM, SSD_STATE)),
                  _resident((CONV_W, CONV_DIM)), _resident((1, CONV_DIM)), _resident((1, N_HEADS)),
                  _resident((1, SSD_DIM)), _resident((1, SSD_DIM)),
                  _resident((N_HEADS, SSD_DIM)), _resident((SSD_DIM, N_HEADS))],
        out_specs=(tile(SSD_DIM), per_b((SSD_DIM, SSD_STATE))),
        out_shape=(jax.ShapeDtypeStruct((b, t, SSD_DIM), BF16),
                   jax.ShapeDtypeStruct((b, SSD_DIM, SSD_STATE), F32)),
        scratch_shapes=[pltpu.VMEM((L + SUBLANES, CONV_DIM), F32),
                        pltpu.VMEM((SSD_DIM, SSD_STATE), F32)],
        compiler_params=_params(("arbitrary", "arbitrary")),
        name="ssd_scan",
    )(xbc, z, dt, conv_prev8, ssm_prev, w_conv, b_conv.reshape(1, CONV_DIM), a_log.reshape(1, N_HEADS),
      jnp.repeat(d_skip, HEAD_DIM).reshape(1, SSD_DIM), g_ssd.reshape(1, SSD_DIM), rexp, rexp.T)


def _outproj_kernel(x_ref, attn_ref, y_ref, wo_ref, o_ref):
    mix = _dot(attn_ref[...], wo_ref[0:FOX_DIM, :]) + _dot(y_ref[...], wo_ref[FOX_DIM:, :])
    o_ref[...] = x_ref[...] + mix


def _outproj(x2d, attn, y, w_out):
    n, d = x2d.shape
    tm = min(512, n)
    assert n % tm == 0
    row = lambda w: pl.BlockSpec((tm, w), lambda i: (i, 0))
    return pl.pallas_call(
        _outproj_kernel,
        grid=(n // tm,),
        in_specs=[row(d), row(FOX_DIM), row(SSD_DIM), _resident((FOX_DIM + SSD_DIM, d))],
        out_specs=row(d),
        out_shape=jax.ShapeDtypeStruct((n, d), F32),
        compiler_params=_params(("arbitrary",)),
        name="outproj",
    )(x2d, attn, y, w_out)


def _ffn_kernel(x_ref, g_ref, wg_ref, wu_ref, wd_ref, o_ref, hf_ref):
    j = pl.program_id(1)

    @pl.when(j == 0)
    def _():
        x = x_ref[...]
        ms = jnp.mean(x * x, axis=-1, keepdims=True)
        hf_ref[...] = (x * lax.rsqrt(ms + EPS) * g_ref[...]).astype(BF16)
        o_ref[...] = x

    hf = hf_ref[...]
    gate = _dot(hf, wg_ref[...])
    act = (gate * _sigmoid(gate) * _dot(hf, wu_ref[...])).astype(BF16)
    o_ref[...] += _dot(act, wd_ref[...])


def _ffn(x2d, g_ffn, w_gate, w_up, w_down):
    n, d = x2d.shape
    dff = w_gate.shape[1]
    tm, tf = min(512, n), 512
    assert n % tm == 0 and dff % tf == 0
    return pl.pallas_call(
        _ffn_kernel,
        grid=(n // tm, dff // tf),
        in_specs=[pl.BlockSpec((tm, d), lambda i, j: (i, 0)),
                  _resident((1, d)),
                  pl.BlockSpec((d, tf), lambda i, j: (0, j)),
                  pl.BlockSpec((d, tf), lambda i, j: (0, j)),
                  pl.BlockSpec((tf, d), lambda i, j: (j, 0))],
        out_specs=pl.BlockSpec((tm, d), lambda i, j: (i, 0)),
        out_shape=jax.ShapeDtypeStruct((n, d), F32),
        scratch_shapes=[pltpu.VMEM((tm, d), BF16)],
        compiler_params=_params(("arbitrary", "arbitrary")),
        name="ffn",
    )(x2d, g_ffn.reshape(1, d), w_gate, w_up, w_down)


def _pack_w_in(w_in):
    d = w_in.shape[0]
    o_f = 3 * FOX_DIM
    o_z = o_f + N_HEADS
    o_xbc = o_z + SSD_DIM
    o_dt = o_xbc + CONV_DIM
    pad = jnp.zeros((d, LANES - N_HEADS), w_in.dtype)
    return jnp.concatenate([w_in[:, :o_f], w_in[:, o_z:o_xbc], w_in[:, o_xbc:o_dt],
                            w_in[:, o_f:o_z], pad, w_in[:, o_dt:], pad], axis=1).astype(BF16)


def _layer(x, conv_prev, ssm_prev, k_prev, v_prev, logf_prev, wts, *, attn_tq, attn_tk, fbias_blk, ssd_chunk):
    (g_mix, w_all, f_bias, g_q, g_k, w_conv, b_conv, dt_bias, a_log, d_skip, g_ssd,
     w_out, g_ffn, w_gate, w_up, w_down) = wts
    b, t, d = x.shape
    n = b * t
    x2d = x.reshape(n, d)
    qb, k, kb, v, vb, z, xbc, logf, dt = _inproj(x2d, g_mix, w_all, g_q, g_k, f_bias, dt_bias)

    past = 0 if k_prev is None else k_prev.shape[1]
    kb3, vb3, logf3 = kb.reshape(b, t, FOX_DIM), vb.reshape(b, t, FOX_DIM), logf.reshape(b, t, N_HEADS)
    logf_all = logf3
    if past:
        kb3 = jnp.concatenate([k_prev.reshape(b, past, FOX_DIM).astype(BF16), kb3], axis=1)
        vb3 = jnp.concatenate([v_prev.reshape(b, past, FOX_DIM).astype(BF16), vb3], axis=1)
        logf_all = jnp.concatenate([logf_prev.astype(F32), logf3], axis=1)
    tkeys = past + t
    tk = attn_tk
    if tkeys % tk:
        tk = -(-tkeys // LANES) * LANES
        grow = ((0, 0), (0, tk - tkeys), (0, 0))
        kb3, vb3, logf_all = jnp.pad(kb3, grow), jnp.pad(vb3, grow), jnp.pad(logf_all, grow)
    kx, qx = _fbias(logf_all, min(fbias_blk, tk))
    attn = _attention(qb.reshape(b, t, FOX_DIM), qx, kb3, kx, jnp.transpose(vb3, (0, 2, 1)),
                      past=past, tq=attn_tq, tk=tk)

    conv_prev8 = jnp.pad(conv_prev.astype(F32), ((0, 0), (SUBLANES - (CONV_W - 1), 0), (0, 0)))
    y, h_last = _ssd(xbc.reshape(b, t, CONV_DIM), z.reshape(b, t, SSD_DIM), dt.reshape(b, t, N_HEADS),
                     conv_prev8, ssm_prev.astype(F32).reshape(b, SSD_DIM, SSD_STATE),
                     w_conv, b_conv, a_log, d_skip, g_ssd, L=ssd_chunk)

    x1 = _outproj(x2d, attn.reshape(n, FOX_DIM), y.reshape(n, SSD_DIM), w_out)
    out = _ffn(x1, g_ffn, w_gate, w_up, w_down)

    xbc_all = jnp.concatenate([conv_prev.astype(F32), xbc.reshape(b, t, CONV_DIM)], axis=1)
    return (out.reshape(b, t, d), xbc_all[:, -(CONV_W - 1):],
            h_last.reshape(b, N_HEADS, HEAD_DIM, SSD_STATE),
            k.reshape(b, t, N_HEADS, HEAD_DIM), v.reshape(b, t, N_HEADS, HEAD_DIM), logf3)


def kernel(x_prompt, x_sample, cache_conv, state_ssm, cache_fox_k, cache_fox_v, cache_fox_logf, g_mix, w_in, f_bias, g_q, g_k, w_conv, b_conv, dt_bias, a_log, d_skip, g_ssd, w_out, g_ffn, w_gate, w_up, w_down):
    depth = g_mix.shape[0]
    yp, ys = x_prompt, x_sample
    bp = x_prompt.shape[0]
    outs_p = [[] for _ in range(5)]
    outs_s = [[] for _ in range(5)]
    for i in range(depth):
        wts = (g_mix[i], _pack_w_in(w_in[i]), f_bias[i], g_q[i], g_k[i], w_conv[i], b_conv[i], dt_bias[i],
               a_log[i], d_skip[i], g_ssd[i], w_out[i].astype(BF16), g_ffn[i],
               w_gate[i].astype(BF16), w_up[i].astype(BF16), w_down[i].astype(BF16))
        tp = yp.shape[1]
        yp, *rest = _layer(
            yp, jnp.zeros((bp, CONV_W - 1, CONV_DIM), F32),
            jnp.zeros((bp, N_HEADS, HEAD_DIM, SSD_STATE), F32), None, None, None, wts,
            attn_tq=min(512, tp), attn_tk=min(1024, tp), fbias_blk=256, ssd_chunk=min(128, tp))
        for lst, val in zip(outs_p, rest):
            lst.append(val)
        tsamp = ys.shape[1]
        ys, *rest = _layer(
            ys, cache_conv[i], state_ssm[i], cache_fox_k[i], cache_fox_v[i], cache_fox_logf[i], wts,
            attn_tq=tsamp, attn_tk=LANES, fbias_blk=LANES, ssd_chunk=tsamp)
        for lst, val in zip(outs_s, rest):
            lst.append(val)
    return (yp, ys, *[jnp.stack(l) for l in outs_p], *[jnp.stack(l) for l in outs_s])
```

```python
import functools

import jax
import jax.numpy as jnp
import numpy as np
from jax import lax
from jax.experimental import pallas as pl
from jax.experimental.pallas import tpu as pltpu

F32 = jnp.float32
BF16 = jnp.bfloat16

EPS = 1e-6
HEAD_DIM = 64
N_HEADS = 16
FOX_DIM = 1024
SSD_DIM = 1024
SSD_GROUPS = 2
SSD_STATE = 128
GROUP_DIM = SSD_DIM // SSD_GROUPS
CONV_W = 4
CONV_DIM = SSD_DIM + 2 * SSD_GROUPS * SSD_STATE
LANES = 128
SUBLANES = 8
MXU_DIM = 256
NEG_BIG = -1e30
LOG2E = 1.4426950408889634
VMEM_LIMIT = 56 * 1024 * 1024

COL_Q, COL_K, COL_V, COL_Z = 0, FOX_DIM, 2 * FOX_DIM, 3 * FOX_DIM
COL_XBC = COL_Z + SSD_DIM
COL_F = COL_XBC + CONV_DIM
COL_DT = COL_F + LANES
IN_COLS = COL_DT + LANES


def _dot(a, b):
    return jnp.dot(a, b, preferred_element_type=F32)


def _dot_nt(a, b):
    return lax.dot_general(a, b, (((1,), (1,)), ((), ())), preferred_element_type=F32)


def _dot_tn(a, b):
    return lax.dot_general(a, b, (((0,), (0,)), ((), ())), preferred_element_type=F32)


def _split3(x):
    x1 = x.astype(BF16)
    r = x - x1.astype(F32)
    x2 = r.astype(BF16)
    r = r - x2.astype(F32)
    return x1, x2, r.astype(BF16)


def _sel_dot(sel, x):
    x1, x2, x3 = _split3(x)
    return _dot(sel, x1) + _dot(sel, x2) + _dot(sel, x3)


def _dot_sel(x, sel):
    x1, x2, x3 = _split3(x)
    return _dot(x1, sel) + _dot(x2, sel) + _dot(x3, sel)


def _sel_dot_nt(sel, x):
    x1, x2, x3 = _split3(x)
    return _dot_nt(sel, x1) + _dot_nt(sel, x2) + _dot_nt(sel, x3)


def _sigmoid(x):
    return 1.0 / (1.0 + jnp.exp(-x))


def _softplus(x):
    return jnp.maximum(x, 0.0) + jnp.log1p(jnp.exp(-jnp.abs(x)))


def _resident(shape):
    nd = len(shape)
    return pl.BlockSpec(shape, lambda *_: (0,) * nd, pipeline_mode=pl.Buffered(1))


def _params(sem):
    return pltpu.CompilerParams(dimension_semantics=sem, vmem_limit_bytes=VMEM_LIMIT)


def _inproj_kernel(x_ref, gmix_ref, w_ref, e_ref, gq_ref, gk_ref, fb_ref, dtb_ref,
                   q_ref, k_ref, kb_ref, v_ref, vb_ref, z_ref, xbc_ref, logf_ref, dt_ref):
    x = x_ref[...]
    ms = jnp.mean(x * x, axis=-1, keepdims=True)
    h = (x * lax.rsqrt(ms + EPS) * gmix_ref[...]).astype(BF16)
    e = e_ref[...]

    def head_rms(y, g):
        ysq = y * y
        hi = ysq.astype(BF16)
        lo = (ysq - hi.astype(F32)).astype(BF16)
        parts = []
        for c in range(FOX_DIM // MXU_DIM):
            sl = slice(MXU_DIM * c, MXU_DIM * (c + 1))
            parts.append(_dot(hi[:, sl], e) + _dot(lo[:, sl], e))
        ss = jnp.concatenate(parts, axis=-1)
        return y * lax.rsqrt(ss * (1.0 / HEAD_DIM) + EPS) * g

    q = _dot(h, w_ref[:, COL_Q:COL_Q + FOX_DIM])
    q_ref[...] = (head_rms(q, gq_ref[...]) * (HEAD_DIM ** -0.5 * LOG2E)).astype(BF16)
    k = head_rms(_dot(h, w_ref[:, COL_K:COL_K + FOX_DIM]), gk_ref[...])
    for hd in range(N_HEADS):
        k_ref[:, hd, :] = k[:, HEAD_DIM * hd:HEAD_DIM * (hd + 1)]
    kb_ref[...] = k.astype(BF16)
    v = _dot(h, w_ref[:, COL_V:COL_V + FOX_DIM])
    for hd in range(N_HEADS):
        v_ref[:, hd, :] = v[:, HEAD_DIM * hd:HEAD_DIM * (hd + 1)]
    vb_ref[...] = v.astype(BF16)
    z_ref[...] = _dot(h, w_ref[:, COL_Z:COL_Z + SSD_DIM])
    xbc_ref[...] = _dot(h, w_ref[:, COL_XBC:COL_XBC + CONV_DIM])
    f_raw = _dot(h, w_ref[:, COL_F:COL_F + LANES])[:, :N_HEADS]
    logf_ref[...] = -_softplus(-(f_raw + fb_ref[...]))
    dt_raw = _dot(h, w_ref[:, COL_DT:COL_DT + LANES])[:, :N_HEADS]
    dt_ref[...] = _softplus(dt_raw + dtb_ref[...])


def _inproj(x2d, g_mix, w_all, g_q, g_k, f_bias, dt_bias):
    n, d = x2d.shape
    tm = min(256, n)
    assert n % tm == 0
    blk = jnp.arange(MXU_DIM) // HEAD_DIM
    e = (blk[:, None] == blk[None, :]).astype(BF16)
    row = lambda w: pl.BlockSpec((tm, w), lambda i: (i, 0))
    heads = pl.BlockSpec((tm, N_HEADS, HEAD_DIM), lambda i: (i, 0, 0))
    out_shapes = (
        jax.ShapeDtypeStruct((n, FOX_DIM), BF16),
        jax.ShapeDtypeStruct((n, N_HEADS, HEAD_DIM), F32),
        jax.ShapeDtypeStruct((n, FOX_DIM), BF16),
        jax.ShapeDtypeStruct((n, N_HEADS, HEAD_DIM), F32),
        jax.ShapeDtypeStruct((n, FOX_DIM), BF16),
        jax.ShapeDtypeStruct((n, SSD_DIM), F32),
        jax.ShapeDtypeStruct((n, CONV_DIM), F32),
        jax.ShapeDtypeStruct((n, N_HEADS), F32),
        jax.ShapeDtypeStruct((n, N_HEADS), F32),
    )
    return pl.pallas_call(
        _inproj_kernel,
        grid=(n // tm,),
        in_specs=[row(d), _resident((1, d)), _resident((d, IN_COLS)), _resident((MXU_DIM, MXU_DIM)),
                  _resident((1, FOX_DIM)), _resident((1, FOX_DIM)),
                  _resident((1, N_HEADS)), _resident((1, N_HEADS))],
        out_specs=(row(FOX_DIM), heads, row(FOX_DIM), heads, row(FOX_DIM),
                   row(SSD_DIM), row(CONV_DIM), row(N_HEADS), row(N_HEADS)),
        out_shape=out_shapes,
        compiler_params=_params(("arbitrary",)),
        name="inproj",
    )(x2d, g_mix.reshape(1, d), w_all, e,
      jnp.tile(g_q, N_HEADS).reshape(1, FOX_DIM), jnp.tile(g_k, N_HEADS).reshape(1, FOX_DIM),
      f_bias.reshape(1, N_HEADS), dt_bias.reshape(1, N_HEADS))


N_EXTRA = 3
Q_EXTRA_SPLIT = 2 * N_EXTRA


def _placement_constants():
    pq = np.zeros((N_EXTRA, N_HEADS, FOX_DIM), np.float32)
    pk = np.zeros((N_EXTRA, N_HEADS, FOX_DIM), np.float32)
    oq = np.zeros((1, FOX_DIM), np.float32)
    ok = np.zeros((1, FOX_DIM), np.float32)
    for h in range(N_HEADS):
        base = LANES * (h // 2)
        for c in range(N_EXTRA):
            if h % 2 == 0:
                pq[c, h, base + c] = 1.0
                pk[c, h, base + N_EXTRA + c] = -1.0
                oq[0, base + N_EXTRA + c] = 1.0
                ok[0, base + c] = 1.0
            else:
                pq[c, h, base + 3 * N_EXTRA + c] = 1.0
                pk[c, h, base + 2 * N_EXTRA + c] = -1.0
                oq[0, base + 2 * N_EXTRA + c] = 1.0
                ok[0, base + 3 * N_EXTRA + c] = 1.0
    return (jnp.asarray(pq, BF16), jnp.asarray(pk, BF16), jnp.asarray(oq), jnp.asarray(ok))


def _fbias_kernel(logf_ref, pq_ref, pk_ref, oq_ref, ok_ref, kx_ref, qx_ref, carry_ref, *, blk, nblk):
    r = lax.broadcasted_iota(jnp.int32, (blk, blk), 0)
    c = lax.broadcasted_iota(jnp.int32, (blk, blk), 1)
    ltri = jnp.where(r >= c, 1.0, 0.0).astype(BF16)

    @pl.when(pl.program_id(1) == 0)
    def _():
        carry_ref[...] = jnp.zeros(carry_ref.shape, F32)

    lane = lax.broadcasted_iota(jnp.int32, (blk, N_EXTRA * N_HEADS), 1)

    def body(i, carry):
        off = pl.multiple_of(i * blk, blk)
        f = _sel_dot(ltri, logf_ref[0, pl.ds(off, blk), :]) + carry
        p1, p2, p3 = _split3(f * LOG2E)
        pieces = jnp.where(lane < N_HEADS, p1, jnp.where(lane < 2 * N_HEADS, p2, p3))
        kx_ref[0, pl.ds(off, blk), :] = (ok_ref[...] + _dot(pieces, pk_ref[...])).astype(BF16)
        qx_ref[0, pl.ds(off, blk), :] = (oq_ref[...] + _dot(pieces, pq_ref[...])).astype(BF16)
        return f[blk - 1:blk, :]

    carry_ref[...] = lax.fori_loop(0, nblk, body, carry_ref[...], unroll=True)


def _fbias(logf3, blk):
    b, t, _ = logf3.shape
    tt = t if t <= 4096 else 2048
    assert t % tt == 0 and tt % blk == 0
    pq, pk, oq, ok = _placement_constants()
    width = N_EXTRA * N_HEADS
    out = jax.ShapeDtypeStruct((b, t, FOX_DIM), BF16)
    tile = lambda w: pl.BlockSpec((1, tt, w), lambda bi, ti: (bi, ti, 0))
    return pl.pallas_call(
        functools.partial(_fbias_kernel, blk=blk, nblk=tt // blk),
        grid=(b, t // tt),
        in_specs=[tile(width), _resident((width, FOX_DIM)), _resident((width, FOX_DIM)),
                  _resident((1, FOX_DIM)), _resident((1, FOX_DIM))],
        out_specs=(tile(FOX_DIM), tile(FOX_DIM)),
        out_shape=(out, out),
        scratch_shapes=[pltpu.VMEM((1, width), F32)],
        compiler_params=_params(("arbitrary", "arbitrary")),
        name="fbias",
    )(jnp.tile(logf3, (1, 1, N_EXTRA)), pq.reshape(width, FOX_DIM), pk.reshape(width, FOX_DIM), oq, ok)


def _attn_kernel(q_ref, qx_ref, k_ref, kx_ref, vt_ref, o_ref, kcat_ref, qs_ref, acc_ref, sa_ref, sb_ref,
                 *, tq, tsub, tk, past):
    i = pl.program_id(2)
    nsub = tq // tsub
    chains = [(sb, a) for sb in range(nsub) for a in range(2)]

    @pl.when(i == 0)
    def _():
        kcat_ref[:, 0:LANES] = k_ref[0]
        kcat_ref[:, LANES:2 * LANES] = kx_ref[0]

    lane = lax.broadcasted_iota(jnp.int32, (tsub, LANES), 1)
    for sb in range(nsub):
        q2 = q_ref[0, tsub * sb:tsub * (sb + 1), :].astype(F32)
        qx2 = qx_ref[0, tsub * sb:tsub * (sb + 1), :].astype(F32)
        zero = jnp.zeros_like(q2)
        for a, keep_q, keep_x in ((0, lane < HEAD_DIM, lane < Q_EXTRA_SPLIT),
                                  (1, lane >= HEAD_DIM, lane >= Q_EXTRA_SPLIT)):
            qs_ref[2 * sb + a, 0:LANES, :] = jnp.where(keep_q, q2, zero).T.astype(BF16)
            qs_ref[2 * sb + a, LANES:2 * LANES, :] = jnp.where(keep_x, qx2, zero).T.astype(BF16)
    acc_ref[...] = jnp.zeros(acc_ref.shape, F32)

    q_lo = past + i * tq
    jm = q_lo // tk

    def qk_stage(j, s_ref, masked):
        off = pl.multiple_of(j * tk, tk)
        kc = kcat_ref[pl.ds(off, tk), :]
        mx = []
        for c, (sb, a) in enumerate(chains):
            s = _dot(kc, qs_ref[c])
            if masked:
                kpos = off + lax.broadcasted_iota(jnp.int32, (tk, tsub), 0)
                qpos = q_lo + tsub * sb + lax.broadcasted_iota(jnp.int32, (tk, tsub), 1)
                s = jnp.where(kpos <= qpos, s, NEG_BIG)
            s_ref[c] = s
            mx.append(jnp.max(s, axis=0, keepdims=True))
        return tuple(mx)

    def sm_stage(j, s_ref, mx, state):
        off = pl.multiple_of(j * tk, tk)
        vts = [vt_ref[0, HEAD_DIM * a:HEAD_DIM * (a + 1), pl.ds(off, tk)] for a in range(2)]
        out, probs = [], []
        for c, (sb, a) in enumerate(chains):
            m_prev, l_prev = state[c]
            m_new = jnp.maximum(m_prev, mx[c])
            alpha = jnp.exp2(m_prev - m_new)
            p = jnp.exp2(s_ref[c] - m_new)
            out.append((m_new, alpha * l_prev + jnp.sum(p, axis=0, keepdims=True)))
            probs.append((alpha, p.astype(BF16)))
        for c, (sb, a) in enumerate(chains):
            alpha, p = probs[c]
            acc_ref[c] = alpha * acc_ref[c] + _dot(vts[a], p)
        return tuple(out)

    init = tuple((jnp.full((1, tsub), NEG_BIG, F32), jnp.zeros((1, tsub), F32)) for _ in chains)
    mx0 = qk_stage(jm, sa_ref, True)

    def pair_body(u, carry):
        jprev, mx_a, state = carry
        mx_b = qk_stage(2 * u, sb_ref, False)
        state = sm_stage(jprev, sa_ref, mx_a, state)
        mx_a = qk_stage(2 * u + 1, sa_ref, False)
        state = sm_stage(2 * u, sb_ref, mx_b, state)
        return 2 * u + 1, mx_a, state

    jprev, mx_a, state = lax.fori_loop(0, jm // 2, pair_body, (jm, mx0, init))

    def odd_tail(args):
        jprev, mx_a, state = args
        mx_b = qk_stage(jm - 1, sb_ref, False)
        state = sm_stage(jprev, sa_ref, mx_a, state)
        return sm_stage(jm - 1, sb_ref, mx_b, state)

    def even_tail(args):
        jprev, mx_a, state = args
        return sm_stage(jprev, sa_ref, mx_a, state)

    state = lax.cond(jm % 2 == 1, odd_tail, even_tail, (jprev, mx_a, state))

    for sb in range(nsub):
        ot = jnp.concatenate([acc_ref[2 * sb + a] * (1.0 / state[2 * sb + a][1]) for a in range(2)], axis=0)
        o_ref[0, tsub * sb:tsub * (sb + 1), :] = ot.T.astype(BF16)


def _attention(q, qx, k, kx, vt, *, past, tq, tk):
    b, t, _ = q.shape
    tkeys = k.shape[1]
    npairs = N_HEADS // 2
    assert t % tq == 0 and tkeys % tk == 0 and past % tq == 0 and tk % tq == 0
    tsub = min(tq, MXU_DIM)
    assert tq % tsub == 0
    nchain = 2 * (tq // tsub)
    qoff = past // tq
    keys = pl.BlockSpec((1, tkeys, LANES), lambda bi, p, i: (bi, 0, p), pipeline_mode=pl.Buffered(1))
    return pl.pallas_call(
        functools.partial(_attn_kernel, tq=tq, tsub=tsub, tk=tk, past=past),
        grid=(b, npairs, t // tq),
        in_specs=[pl.BlockSpec((1, tq, LANES), lambda bi, p, i: (bi, i, p)),
                  pl.BlockSpec((1, tq, LANES), lambda bi, p, i: (bi, qoff + i, p)),
                  keys, keys,
                  pl.BlockSpec((1, LANES, tkeys), lambda bi, p, i: (bi, p, 0))],
        out_specs=pl.BlockSpec((1, tq, LANES), lambda bi, p, i: (bi, i, p)),
        out_shape=jax.ShapeDtypeStruct((b, t, FOX_DIM), BF16),
        scratch_shapes=[pltpu.VMEM((tkeys, 2 * LANES), BF16),
                        pltpu.VMEM((nchain, 2 * LANES, tsub), BF16),
                        pltpu.VMEM((nchain, HEAD_DIM, tsub), F32),
                        pltpu.VMEM((nchain, tk, tsub), F32),
                        pltpu.VMEM((nchain, tk, tsub), F32)],
        compiler_params=_params(("arbitrary", "arbitrary", "arbitrary")),
        name="fox_attention",
    )(q, qx, k, kx, vt)


def _ssd_kernel(xbc_ref, z_ref, dt_ref, cprev_ref, sprev_ref, wconv_ref, bconv_ref, alog_ref,
                dskip_ref, gssd_ref, rexp_ref, rexpt_ref,
                y_ref, hfin_ref, ext_ref, state_ref, *, L):
    c = pl.program_id(1)

    @pl.when(c == 0)
    def _():
        state_ref[...] = sprev_ref[0]
        ext_ref[0:SUBLANES, :] = cprev_ref[0]

    x_tile = xbc_ref[0]
    ext_ref[SUBLANES:SUBLANES + L, :] = x_tile
    w = wconv_ref[...]
    pre = bconv_ref[...]
    for j in range(CONV_W):
        start = SUBLANES - (CONV_W - 1) + j
        pre = pre + ext_ref[start:start + L, :] * w[j:j + 1, :]
    ext_ref[0:SUBLANES, :] = x_tile[L - SUBLANES:L, :]
    u = pre * _sigmoid(pre)
    xs = u[:, 0:SSD_DIM]
    bmat = u[:, SSD_DIM:SSD_DIM + SSD_GROUPS * SSD_STATE].astype(BF16)
    cmat = u[:, SSD_DIM + SSD_GROUPS * SSD_STATE:].astype(BF16)

    dt = dt_ref[0]
    a = dt * (-jnp.exp(alog_ref[...]))
    r = lax.broadcasted_iota(jnp.int32, (L, L), 0)
    cc = lax.broadcasted_iota(jnp.int32, (L, L), 1)
    causal = r >= cc
    ltri = jnp.where(causal, 1.0, 0.0).astype(BF16)
    cum = _sel_dot(ltri, a)
    e_r = lax.broadcasted_iota(jnp.int32, (N_HEADS, N_HEADS), 0)
    e_c = lax.broadcasted_iota(jnp.int32, (N_HEADS, N_HEADS), 1)
    eye = jnp.where(e_r == e_c, 1.0, 0.0).astype(BF16)
    cum_t = _sel_dot_nt(eye, cum)
    last = cum[L - 1:L, :]

    rexp = rexp_ref[...]
    dtx = _dot_sel(dt, rexp)
    wendx = _dot_sel(jnp.exp(last - cum), rexp)
    ecumx = _dot_sel(jnp.exp(cum), rexp)
    xdt = xs * dtx
    xdt_b = xdt.astype(BF16)
    xw_b = (xdt * wendx).astype(BF16)
    elast = jnp.exp(cum_t[:, L - 1:L])
    elast_b = jnp.broadcast_to(elast, (N_HEADS, SSD_STATE))
    rexpt = rexpt_ref[...]
    lane = lax.broadcasted_iota(jnp.int32, (L, LANES), 1)

    heads_per_group = N_HEADS // SSD_GROUPS
    ys = []
    for g in range(SSD_GROUPS):
        gs = slice(GROUP_DIM * g, GROUP_DIM * (g + 1))
        cg = cmat[:, SSD_STATE * g:SSD_STATE * (g + 1)]
        bg = bmat[:, SSD_STATE * g:SSD_STATE * (g + 1)]
        cb = _dot_nt(cg, bg)
        st = state_ref[gs, :]
        yoff = _dot_nt(cg, st.astype(BF16))
        for pr in range(heads_per_group // 2):
            ps = slice(GROUP_DIM * g + LANES * pr, GROUP_DIM * g + LANES * (pr + 1))
            slab = xdt_b[:, ps]
            res = []
            for hh in range(2):
                hd = heads_per_group * g + 2 * pr + hh
                seg = cum[:, hd:hd + 1] - cum_t[hd:hd + 1, :]
                dec = jnp.exp(jnp.where(causal, seg, -jnp.inf))
                res.append(_dot((cb * dec).astype(BF16), slab))
            ydiag = jnp.where(lane < HEAD_DIM, res[0], res[1])
            ys.append(ydiag + yoff[:, LANES * pr:LANES * (pr + 1)] * ecumx[:, ps])
        snew = _dot_tn(xw_b[:, gs], bg)
        scale = _sel_dot(rexpt[gs, :], elast_b)
        state_ref[gs, :] = scale * st + snew

    y = jnp.concatenate(ys, axis=-1) + dskip_ref[...] * xs
    zt = z_ref[0]
    ug = y * (zt * _sigmoid(zt))
    outs = []
    for g in range(SSD_GROUPS):
        ugg = ug[:, GROUP_DIM * g:GROUP_DIM * (g + 1)]
        outs.append(ugg * lax.rsqrt(jnp.mean(ugg * ugg, axis=-1, keepdims=True) + EPS))
    y_ref[0] = (jnp.concatenate(outs, axis=-1) * gssd_ref[...]).astype(BF16)

    @pl.when(c == pl.num_programs(1) - 1)
    def _():
        hfin_ref[0] = state_ref[...]


def _ssd(xbc, z, dt, conv_prev8, ssm_prev, w_conv, b_conv, a_log, d_skip, g_ssd, *, L):
    b, t, _ = xbc.shape
    assert t % L == 0 and L % SUBLANES == 0
    head_of_lane = jnp.arange(SSD_DIM) // HEAD_DIM
    rexp = (jnp.arange(N_HEADS)[:, None] == head_of_lane[None, :]).astype(BF16)
    tile = lambda w: pl.BlockSpec((1, L, w), lambda bi, ci: (bi, ci, 0))
    per_b = lambda s: pl.BlockSpec((1,) + s, lambda bi, ci: (bi, 0, 0))
    return pl.pallas_call(
        functools.partial(_ssd_kernel, L=L),
        grid=(b, t // L),
        in_specs=[tile(CONV_DIM), tile(SSD_DIM), tile(N_HEADS),
                  per_b((SUBLANES, CONV_DIM)), per_b((SSD_DIM, SSD_STATE)),
                  _resident((CONV_W, CONV_DIM)), _resident((1, CONV_DIM)), _resident((1, N_HEADS)),
                  _resident((1, SSD_DIM)), _resident((1, SSD_DIM)),
                  _resident((N_HEADS, SSD_DIM)), _resident((SSD_DIM, N_HEADS))],
        out_specs=(tile(SSD_DIM), per_b((SSD_DIM, SSD_STATE))),
        out_shape=(jax.ShapeDtypeStruct((b, t, SSD_DIM), BF16),
                   jax.ShapeDtypeStruct((b, SSD_DIM, SSD_STATE), F32)),
        scratch_shapes=[pltpu.VMEM((L + SUBLANES, CONV_DIM), F32),
                        pltpu.VMEM((SSD_DIM, SSD_STATE), F32)],
        compiler_params=_params(("arbitrary", "arbitrary")),
        name="ssd_scan",
    )(xbc, z, dt, conv_prev8, ssm_prev, w_conv, b_conv.reshape(1, CONV_DIM), a_log.reshape(1, N_HEADS),
      jnp.repeat(d_skip, HEAD_DIM).reshape(1, SSD_DIM), g_ssd.reshape(1, SSD_DIM), rexp, rexp.T)


def _outproj_kernel(x_ref, attn_ref, y_ref, wo_ref, o_ref):
    mix = _dot(attn_ref[...], wo_ref[0:FOX_DIM, :]) + _dot(y_ref[...], wo_ref[FOX_DIM:, :])
    o_ref[...] = x_ref[...] + mix


def _outproj(x2d, attn, y, w_out):
    n, d = x2d.shape
    tm = min(512, n)
    assert n % tm == 0
    row = lambda w: pl.BlockSpec((tm, w), lambda i: (i, 0))
    return pl.pallas_call(
        _outproj_kernel,
        grid=(n // tm,),
        in_specs=[row(d), row(FOX_DIM), row(SSD_DIM), _resident((FOX_DIM + SSD_DIM, d))],
        out_specs=row(d),
        out_shape=jax.ShapeDtypeStruct((n, d), F32),
        compiler_params=_params(("arbitrary",)),
        name="outproj",
    )(x2d, attn, y, w_out)


def _ffn_kernel(x_ref, g_ref, wg_ref, wu_ref, wd_ref, o_ref, hf_ref):
    j = pl.program_id(1)

    @pl.when(j == 0)
    def _():
        x = x_ref[...]
        ms = jnp.mean(x * x, axis=-1, keepdims=True)
        hf_ref[...] = (x * lax.rsqrt(ms + EPS) * g_ref[...]).astype(BF16)
        o_ref[...] = x

    hf = hf_ref[...]
    gate = _dot(hf, wg_ref[...])
    act = (gate * _sigmoid(gate) * _dot(hf, wu_ref[...])).astype(BF16)
    o_ref[...] += _dot(act, wd_ref[...])


def _ffn(x2d, g_ffn, w_gate, w_up, w_down):
    n, d = x2d.shape
    dff = w_gate.shape[1]
    tm, tf = min(512, n), 512
    assert n % tm == 0 and dff % tf == 0
    return pl.pallas_call(
        _ffn_kernel,
        grid=(n // tm, dff // tf),
        in_specs=[pl.BlockSpec((tm, d), lambda i, j: (i, 0)),
                  _resident((1, d)),
                  pl.BlockSpec((d, tf), lambda i, j: (0, j)),
                  pl.BlockSpec((d, tf), lambda i, j: (0, j)),
                  pl.BlockSpec((tf, d), lambda i, j: (j, 0))],
        out_specs=pl.BlockSpec((tm, d), lambda i, j: (i, 0)),
        out_shape=jax.ShapeDtypeStruct((n, d), F32),
        scratch_shapes=[pltpu.VMEM((tm, d), BF16)],
        compiler_params=_params(("arbitrary", "arbitrary")),
        name="ffn",
    )(x2d, g_ffn.reshape(1, d), w_gate, w_up, w_down)


def _pack_w_in(w_in):
    d = w_in.shape[0]
    o_f = 3 * FOX_DIM
    o_z = o_f + N_HEADS
    o_xbc = o_z + SSD_DIM
    o_dt = o_xbc + CONV_DIM
    pad = jnp.zeros((d, LANES - N_HEADS), w_in.dtype)
    return jnp.concatenate([w_in[:, :o_f], w_in[:, o_z:o_xbc], w_in[:, o_xbc:o_dt],
                            w_in[:, o_f:o_z], pad, w_in[:, o_dt:], pad], axis=1).astype(BF16)


def _layer(x, conv_prev, ssm_prev, k_prev, v_prev, logf_prev, wts, *, attn_tq, attn_tk, fbias_blk, ssd_chunk):
    (g_mix, w_all, f_bias, g_q, g_k, w_conv, b_conv, dt_bias, a_log, d_skip, g_ssd,
     w_out, g_ffn, w_gate, w_up, w_down) = wts
    b, t, d = x.shape
    n = b * t
    x2d = x.reshape(n, d)
    qb, k, kb, v, vb, z, xbc, logf, dt = _inproj(x2d, g_mix, w_all, g_q, g_k, f_bias, dt_bias)

    past = 0 if k_prev is None else k_prev.shape[1]
    kb3, vb3, logf3 = kb.reshape(b, t, FOX_DIM), vb.reshape(b, t, FOX_DIM), logf.reshape(b, t, N_HEADS)
    logf_all = logf3
    if past:
        kb3 = jnp.concatenate([k_prev.reshape(b, past, FOX_DIM).astype(BF16), kb3], axis=1)
        vb3 = jnp.concatenate([v_prev.reshape(b, past, FOX_DIM).astype(BF16), vb3], axis=1)
        logf_all = jnp.concatenate([logf_prev.astype(F32), logf3], axis=1)
    tkeys = past + t
    tk = attn_tk
    if tkeys % tk:
        tk = -(-tkeys // LANES) * LANES
        grow = ((0, 0), (0, tk - tkeys), (0, 0))
        kb3, vb3, logf_all = jnp.pad(kb3, grow), jnp.pad(vb3, grow), jnp.pad(logf_all, grow)
    kx, qx = _fbias(logf_all, min(fbias_blk, tk))
    attn = _attention(qb.reshape(b, t, FOX_DIM), qx, kb3, kx, jnp.transpose(vb3, (0, 2, 1)),
                      past=past, tq=attn_tq, tk=tk)

    conv_prev8 = jnp.pad(conv_prev.astype(F32), ((0, 0), (SUBLANES - (CONV_W - 1), 0), (0, 0)))
    y, h_last = _ssd(xbc.reshape(b, t, CONV_DIM), z.reshape(b, t, SSD_DIM), dt.reshape(b, t, N_HEADS),
                     conv_prev8, ssm_prev.astype(F32).reshape(b, SSD_DIM, SSD_STATE),
                     w_conv, b_conv, a_log, d_skip, g_ssd, L=ssd_chunk)

    x1 = _outproj(x2d, attn.reshape(n, FOX_DIM), y.reshape(n, SSD_DIM), w_out)
    out = _ffn(x1, g_ffn, w_gate, w_up, w_down)

    xbc_all = jnp.concatenate([conv_prev.astype(F32), xbc.reshape(b, t, CONV_DIM)], axis=1)
    return (out.reshape(b, t, d), xbc_all[:, -(CONV_W - 1):],
            h_last.reshape(b, N_HEADS, HEAD_DIM, SSD_STATE),
            k.reshape(b, t, N_HEADS, HEAD_DIM), v.reshape(b, t, N_HEADS, HEAD_DIM), logf3)


def kernel(x_prompt, x_sample, cache_conv, state_ssm, cache_fox_k, cache_fox_v, cache_fox_logf, g_mix, w_in, f_bias, g_q, g_k, w_conv, b_conv, dt_bias, a_log, d_skip, g_ssd, w_out, g_ffn, w_gate, w_up, w_down):
    depth = g_mix.shape[0]
    yp, ys = x_prompt, x_sample
    bp = x_prompt.shape[0]
    outs_p = [[] for _ in range(5)]
    outs_s = [[] for _ in range(5)]
    for i in range(depth):
        wts = (g_mix[i], _pack_w_in(w_in[i]), f_bias[i], g_q[i], g_k[i], w_conv[i], b_conv[i], dt_bias[i],
               a_log[i], d_skip[i], g_ssd[i], w_out[i].astype(BF16), g_ffn[i],
               w_gate[i].astype(BF16), w_up[i].astype(BF16), w_down[i].astype(BF16))
        tp = yp.shape[1]
        yp, *rest = _layer(
            yp, jnp.zeros((bp, CONV_W - 1, CONV_DIM), F32),
            jnp.zeros((bp, N_HEADS, HEAD_DIM, SSD_STATE), F32), None, None, None, wts,
            attn_tq=min(1024, tp), attn_tk=min(1024, tp), fbias_blk=256, ssd_chunk=min(128, tp))
        for lst, val in zip(outs_p, rest):
            lst.append(val)
        tsamp = ys.shape[1]
        ys, *rest = _layer(
            ys, cache_conv[i], state_ssm[i], cache_fox_k[i], cache_fox_v[i], cache_fox_logf[i], wts,
            attn_tq=tsamp, attn_tk=LANES, fbias_blk=LANES, ssd_chunk=tsamp)
        for lst, val in zip(outs_s, rest):
            lst.append(val)
    return (yp, ys, *[jnp.stack(l) for l in outs_p], *[jnp.stack(l) for l in outs_s])
```

```python
import functools

import jax
import jax.numpy as jnp
import numpy as np
from jax import lax
from jax.experimental import pallas as pl
from jax.experimental.pallas import tpu as pltpu

F32 = jnp.float32
BF16 = jnp.bfloat16

EPS = 1e-6
HEAD_DIM = 64
N_HEADS = 16
FOX_DIM = 1024
SSD_DIM = 1024
SSD_GROUPS = 2
SSD_STATE = 128
GROUP_DIM = SSD_DIM // SSD_GROUPS
CONV_W = 4
CONV_DIM = SSD_DIM + 2 * SSD_GROUPS * SSD_STATE
LANES = 128
SUBLANES = 8
MXU_DIM = 256
NEG_BIG = -1e30
LOG2E = 1.4426950408889634
VMEM_LIMIT = 56 * 1024 * 1024

COL_Q, COL_K, COL_V, COL_Z = 0, FOX_DIM, 2 * FOX_DIM, 3 * FOX_DIM
COL_XBC = COL_Z + SSD_DIM
COL_F = COL_XBC + CONV_DIM
COL_DT = COL_F + LANES
IN_COLS = COL_DT + LANES


def _dot(a, b):
    return jnp.dot(a, b, preferred_element_type=F32)


def _dot_nt(a, b):
    return lax.dot_general(a, b, (((1,), (1,)), ((), ())), preferred_element_type=F32)


def _dot_tn(a, b):
    return lax.dot_general(a, b, (((0,), (0,)), ((), ())), preferred_element_type=F32)


def _split3(x):
    x1 = x.astype(BF16)
    r = x - x1.astype(F32)
    x2 = r.astype(BF16)
    r = r - x2.astype(F32)
    return x1, x2, r.astype(BF16)


def _sel_dot(sel, x):
    x1, x2, x3 = _split3(x)
    return _dot(sel, x1) + _dot(sel, x2) + _dot(sel, x3)


def _dot_sel(x, sel):
    x1, x2, x3 = _split3(x)
    return _dot(x1, sel) + _dot(x2, sel) + _dot(x3, sel)


def _sel_dot_nt(sel, x):
    x1, x2, x3 = _split3(x)
    return _dot_nt(sel, x1) + _dot_nt(sel, x2) + _dot_nt(sel, x3)


def _sigmoid(x):
    return 0.5 * jnp.tanh(0.5 * x) + 0.5


def _softplus(x):
    return jnp.maximum(x, 0.0) + jnp.log1p(jnp.exp(-jnp.abs(x)))


def _resident(shape):
    nd = len(shape)
    return pl.BlockSpec(shape, lambda *_: (0,) * nd, pipeline_mode=pl.Buffered(1))


def _params(sem):
    return pltpu.CompilerParams(dimension_semantics=sem, vmem_limit_bytes=VMEM_LIMIT)


def _inproj_kernel(x_ref, gmix_ref, w_ref, e_ref, gq_ref, gk_ref, fb_ref, dtb_ref,
                   q_ref, k_ref, kb_ref, v_ref, vb_ref, z_ref, xbc_ref, logf_ref, dt_ref):
    x = x_ref[...]
    ms = jnp.mean(x * x, axis=-1, keepdims=True)
    h = (x * lax.rsqrt(ms + EPS) * gmix_ref[...]).astype(BF16)
    e = e_ref[...]

    def head_rms(y, g):
        ysq = y * y
        hi = ysq.astype(BF16)
        lo = (ysq - hi.astype(F32)).astype(BF16)
        parts = []
        for c in range(FOX_DIM // MXU_DIM):
            sl = slice(MXU_DIM * c, MXU_DIM * (c + 1))
            parts.append(_dot(hi[:, sl], e) + _dot(lo[:, sl], e))
        ss = jnp.concatenate(parts, axis=-1)
        return y * lax.rsqrt(ss * (1.0 / HEAD_DIM) + EPS) * g

    q = _dot(h, w_ref[:, COL_Q:COL_Q + FOX_DIM])
    q_ref[...] = (head_rms(q, gq_ref[...]) * (HEAD_DIM ** -0.5 * LOG2E)).astype(BF16)
    k = head_rms(_dot(h, w_ref[:, COL_K:COL_K + FOX_DIM]), gk_ref[...])
    for hd in range(N_HEADS):
        k_ref[:, hd, :] = k[:, HEAD_DIM * hd:HEAD_DIM * (hd + 1)]
    kb_ref[...] = k.astype(BF16)
    v = _dot(h, w_ref[:, COL_V:COL_V + FOX_DIM])
    for hd in range(N_HEADS):
        v_ref[:, hd, :] = v[:, HEAD_DIM * hd:HEAD_DIM * (hd + 1)]
    vb_ref[...] = v.astype(BF16)
    z_ref[...] = _dot(h, w_ref[:, COL_Z:COL_Z + SSD_DIM])
    xbc_ref[...] = _dot(h, w_ref[:, COL_XBC:COL_XBC + CONV_DIM])
    f_raw = _dot(h, w_ref[:, COL_F:COL_F + LANES])[:, :N_HEADS]
    logf_ref[...] = -_softplus(-(f_raw + fb_ref[...]))
    dt_raw = _dot(h, w_ref[:, COL_DT:COL_DT + LANES])[:, :N_HEADS]
    dt_ref[...] = _softplus(dt_raw + dtb_ref[...])


def _inproj(x2d, g_mix, w_all, g_q, g_k, f_bias, dt_bias):
    n, d = x2d.shape
    tm = min(256, n)
    assert n % tm == 0
    blk = jnp.arange(MXU_DIM) // HEAD_DIM
    e = (blk[:, None] == blk[None, :]).astype(BF16)
    row = lambda w: pl.BlockSpec((tm, w), lambda i: (i, 0))
    heads = pl.BlockSpec((tm, N_HEADS, HEAD_DIM), lambda i: (i, 0, 0))
    out_shapes = (
        jax.ShapeDtypeStruct((n, FOX_DIM), BF16),
        jax.ShapeDtypeStruct((n, N_HEADS, HEAD_DIM), F32),
        jax.ShapeDtypeStruct((n, FOX_DIM), BF16),
        jax.ShapeDtypeStruct((n, N_HEADS, HEAD_DIM), F32),
        jax.ShapeDtypeStruct((n, FOX_DIM), BF16),
        jax.ShapeDtypeStruct((n, SSD_DIM), F32),
        jax.ShapeDtypeStruct((n, CONV_DIM), F32),
        jax.ShapeDtypeStruct((n, N_HEADS), F32),
        jax.ShapeDtypeStruct((n, N_HEADS), F32),
    )
    return pl.pallas_call(
        _inproj_kernel,
        grid=(n // tm,),
        in_specs=[row(d), _resident((1, d)), _resident((d, IN_COLS)), _resident((MXU_DIM, MXU_DIM)),
                  _resident((1, FOX_DIM)), _resident((1, FOX_DIM)),
                  _resident((1, N_HEADS)), _resident((1, N_HEADS))],
        out_specs=(row(FOX_DIM), heads, row(FOX_DIM), heads, row(FOX_DIM),
                   row(SSD_DIM), row(CONV_DIM), row(N_HEADS), row(N_HEADS)),
        out_shape=out_shapes,
        compiler_params=_params(("arbitrary",)),
        name="inproj",
    )(x2d, g_mix.reshape(1, d), w_all, e,
      jnp.tile(g_q, N_HEADS).reshape(1, FOX_DIM), jnp.tile(g_k, N_HEADS).reshape(1, FOX_DIM),
      f_bias.reshape(1, N_HEADS), dt_bias.reshape(1, N_HEADS))


N_EXTRA = 3
Q_EXTRA_SPLIT = 2 * N_EXTRA


def _placement_constants():
    pq = np.zeros((N_EXTRA, N_HEADS, FOX_DIM), np.float32)
    pk = np.zeros((N_EXTRA, N_HEADS, FOX_DIM), np.float32)
    oq = np.zeros((1, FOX_DIM), np.float32)
    ok = np.zeros((1, FOX_DIM), np.float32)
    for h in range(N_HEADS):
        base = LANES * (h // 2)
        for c in range(N_EXTRA):
            if h % 2 == 0:
                pq[c, h, base + c] = 1.0
                pk[c, h, base + N_EXTRA + c] = -1.0
                oq[0, base + N_EXTRA + c] = 1.0
                ok[0, base + c] = 1.0
            else:
                pq[c, h, base + 3 * N_EXTRA + c] = 1.0
                pk[c, h, base + 2 * N_EXTRA + c] = -1.0
                oq[0, base + 2 * N_EXTRA + c] = 1.0
                ok[0, base + 3 * N_EXTRA + c] = 1.0
    return (jnp.asarray(pq, BF16), jnp.asarray(pk, BF16), jnp.asarray(oq), jnp.asarray(ok))


def _fbias_kernel(logf_ref, pq_ref, pk_ref, oq_ref, ok_ref, kx_ref, qx_ref, carry_ref, *, blk, nblk):
    r = lax.broadcasted_iota(jnp.int32, (blk, blk), 0)
    c = lax.broadcasted_iota(jnp.int32, (blk, blk), 1)
    ltri = jnp.where(r >= c, 1.0, 0.0).astype(BF16)

    @pl.when(pl.program_id(1) == 0)
    def _():
        carry_ref[...] = jnp.zeros(carry_ref.shape, F32)

    lane = lax.broadcasted_iota(jnp.int32, (blk, N_EXTRA * N_HEADS), 1)

    def body(i, carry):
        off = pl.multiple_of(i * blk, blk)
        f = _sel_dot(ltri, logf_ref[0, pl.ds(off, blk), :]) + carry
        p1, p2, p3 = _split3(f * LOG2E)
        pieces = jnp.where(lane < N_HEADS, p1, jnp.where(lane < 2 * N_HEADS, p2, p3))
        kx_ref[0, pl.ds(off, blk), :] = (ok_ref[...] + _dot(pieces, pk_ref[...])).astype(BF16)
        qx_ref[0, pl.ds(off, blk), :] = (oq_ref[...] + _dot(pieces, pq_ref[...])).astype(BF16)
        return f[blk - 1:blk, :]

    carry_ref[...] = lax.fori_loop(0, nblk, body, carry_ref[...], unroll=True)


def _fbias(logf3, blk):
    b, t, _ = logf3.shape
    tt = t if t <= 4096 else 2048
    assert t % tt == 0 and tt % blk == 0
    pq, pk, oq, ok = _placement_constants()
    width = N_EXTRA * N_HEADS
    out = jax.ShapeDtypeStruct((b, t, FOX_DIM), BF16)
    tile = lambda w: pl.BlockSpec((1, tt, w), lambda bi, ti: (bi, ti, 0))
    return pl.pallas_call(
        functools.partial(_fbias_kernel, blk=blk, nblk=tt // blk),
        grid=(b, t // tt),
        in_specs=[tile(width), _resident((width, FOX_DIM)), _resident((width, FOX_DIM)),
                  _resident((1, FOX_DIM)), _resident((1, FOX_DIM))],
        out_specs=(tile(FOX_DIM), tile(FOX_DIM)),
        out_shape=(out, out),
        scratch_shapes=[pltpu.VMEM((1, width), F32)],
        compiler_params=_params(("arbitrary", "arbitrary")),
        name="fbias",
    )(jnp.tile(logf3, (1, 1, N_EXTRA)), pq.reshape(width, FOX_DIM), pk.reshape(width, FOX_DIM), oq, ok)


def _attn_kernel(q_ref, qx_ref, k_ref, kx_ref, vt_ref, o_ref, kcat_ref, qs_ref, acc_ref, sa_ref, sb_ref,
                 *, tq, tsub, tk, past):
    i = pl.program_id(2)
    nsub = tq // tsub
    chains = [(sb, a) for sb in range(nsub) for a in range(2)]

    @pl.when(i == 0)
    def _():
        kcat_ref[:, 0:LANES] = k_ref[0]
        kcat_ref[:, LANES:2 * LANES] = kx_ref[0]

    lane = lax.broadcasted_iota(jnp.int32, (tsub, LANES), 1)
    for sb in range(nsub):
        q2 = q_ref[0, tsub * sb:tsub * (sb + 1), :].astype(F32)
        qx2 = qx_ref[0, tsub * sb:tsub * (sb + 1), :].astype(F32)
        zero = jnp.zeros_like(q2)
        for a, keep_q, keep_x in ((0, lane < HEAD_DIM, lane < Q_EXTRA_SPLIT),
                                  (1, lane >= HEAD_DIM, lane >= Q_EXTRA_SPLIT)):
            qs_ref[2 * sb + a, 0:LANES, :] = jnp.where(keep_q, q2, zero).T.astype(BF16)
            qs_ref[2 * sb + a, LANES:2 * LANES, :] = jnp.where(keep_x, qx2, zero).T.astype(BF16)
    acc_ref[...] = jnp.zeros(acc_ref.shape, F32)

    q_lo = past + i * tq
    jm = q_lo // tk

    def qk_stage(j, s_ref, masked):
        off = pl.multiple_of(j * tk, tk)
        kc = kcat_ref[pl.ds(off, tk), :]
        mx = []
        for c, (sb, a) in enumerate(chains):
            s = _dot(kc, qs_ref[c])
            if masked:
                kpos = off + lax.broadcasted_iota(jnp.int32, (tk, tsub), 0)
                qpos = q_lo + tsub * sb + lax.broadcasted_iota(jnp.int32, (tk, tsub), 1)
                s = jnp.where(kpos <= qpos, s, NEG_BIG)
            s_ref[c] = s
            mx.append(jnp.max(s, axis=0, keepdims=True))
        return tuple(mx)

    def sm_stage(j, s_ref, mx, state):
        off = pl.multiple_of(j * tk, tk)
        vts = [vt_ref[0, HEAD_DIM * a:HEAD_DIM * (a + 1), pl.ds(off, tk)] for a in range(2)]
        out, probs = [], []
        for c, (sb, a) in enumerate(chains):
            m_prev, l_prev = state[c]
            m_new = jnp.maximum(m_prev, mx[c])
            alpha = jnp.exp2(m_prev - m_new)
            p = jnp.exp2(s_ref[c] - m_new)
            out.append((m_new, alpha * l_prev + jnp.sum(p, axis=0, keepdims=True)))
            probs.append((alpha, p.astype(BF16)))
        for c, (sb, a) in enumerate(chains):
            alpha, p = probs[c]
            acc_ref[c] = alpha * acc_ref[c] + _dot(vts[a], p)
        return tuple(out)

    init = tuple((jnp.full((1, tsub), NEG_BIG, F32), jnp.zeros((1, tsub), F32)) for _ in chains)
    mx0 = qk_stage(jm, sa_ref, True)

    def pair_body(u, carry):
        jprev, mx_a, state = carry
        mx_b = qk_stage(2 * u, sb_ref, False)
        state = sm_stage(jprev, sa_ref, mx_a, state)
        mx_a = qk_stage(2 * u + 1, sa_ref, False)
        state = sm_stage(2 * u, sb_ref, mx_b, state)
        return 2 * u + 1, mx_a, state

    jprev, mx_a, state = lax.fori_loop(0, jm // 2, pair_body, (jm, mx0, init))

    def odd_tail(args):
        jprev, mx_a, state = args
        mx_b = qk_stage(jm - 1, sb_ref, False)
        state = sm_stage(jprev, sa_ref, mx_a, state)
        return sm_stage(jm - 1, sb_ref, mx_b, state)

    def even_tail(args):
        jprev, mx_a, state = args
        return sm_stage(jprev, sa_ref, mx_a, state)

    state = lax.cond(jm % 2 == 1, odd_tail, even_tail, (jprev, mx_a, state))

    for sb in range(nsub):
        ot = jnp.concatenate([acc_ref[2 * sb + a] * (1.0 / state[2 * sb + a][1]) for a in range(2)], axis=0)
        o_ref[0, tsub * sb:tsub * (sb + 1), :] = ot.T.astype(BF16)


def _attention(q, qx, k, kx, vt, *, past, tq, tk):
    b, t, _ = q.shape
    tkeys = k.shape[1]
    npairs = N_HEADS // 2
    assert t % tq == 0 and tkeys % tk == 0 and past % tq == 0 and tk % tq == 0
    tsub = min(tq, MXU_DIM)
    assert tq % tsub == 0
    nchain = 2 * (tq // tsub)
    qoff = past // tq
    keys = pl.BlockSpec((1, tkeys, LANES), lambda bi, p, i: (bi, 0, p),
                        pipeline_mode=pl.Buffered(1 if t // tq > 1 else 2))
    return pl.pallas_call(
        functools.partial(_attn_kernel, tq=tq, tsub=tsub, tk=tk, past=past),
        grid=(b, npairs, t // tq),
        in_specs=[pl.BlockSpec((1, tq, LANES), lambda bi, p, i: (bi, i, p)),
                  pl.BlockSpec((1, tq, LANES), lambda bi, p, i: (bi, qoff + i, p)),
                  keys, keys,
                  pl.BlockSpec((1, LANES, tkeys), lambda bi, p, i: (bi, p, 0))],
        out_specs=pl.BlockSpec((1, tq, LANES), lambda bi, p, i: (bi, i, p)),
        out_shape=jax.ShapeDtypeStruct((b, t, FOX_DIM), BF16),
        scratch_shapes=[pltpu.VMEM((tkeys, 2 * LANES), BF16),
                        pltpu.VMEM((nchain, 2 * LANES, tsub), BF16),
                        pltpu.VMEM((nchain, HEAD_DIM, tsub), F32),
                        pltpu.VMEM((nchain, tk, tsub), F32),
                        pltpu.VMEM((nchain, tk, tsub), F32)],
        compiler_params=_params(("arbitrary", "arbitrary", "arbitrary")),
        name="fox_attention",
    )(q, qx, k, kx, vt)


def _ssd_kernel(xbc_ref, z_ref, dt_ref, cprev_ref, sprev_ref, wconv_ref, bconv_ref, alog_ref,
                dskip_ref, gssd_ref, rexp_ref, rexpt_ref,
                y_ref, hfin_ref, ext_ref, state_ref, *, L):
    c = pl.program_id(1)

    @pl.when(c == 0)
    def _():
        state_ref[...] = sprev_ref[0]
        ext_ref[...] = cprev_ref[0]

    x_tile = xbc_ref[0]
    ext = jnp.concatenate([ext_ref[...], x_tile], axis=0)
    w = wconv_ref[...]
    pre = bconv_ref[...] + x_tile * w[CONV_W - 1:CONV_W, :]
    for back in range(1, CONV_W):
        shifted = pltpu.roll(ext, back, axis=0)[SUBLANES:, :]
        pre = pre + shifted * w[CONV_W - 1 - back:CONV_W - back, :]
    ext_ref[...] = x_tile[L - SUBLANES:L, :]
    u = pre * _sigmoid(pre)
    xs = u[:, 0:SSD_DIM]
    bmat = u[:, SSD_DIM:SSD_DIM + SSD_GROUPS * SSD_STATE].astype(BF16)
    cmat = u[:, SSD_DIM + SSD_GROUPS * SSD_STATE:].astype(BF16)

    dt = dt_ref[0]
    a = dt * (-jnp.exp(alog_ref[...]))
    r = lax.broadcasted_iota(jnp.int32, (L, L), 0)
    cc = lax.broadcasted_iota(jnp.int32, (L, L), 1)
    causal = r >= cc
    ltri = jnp.where(causal, 1.0, 0.0).astype(BF16)
    cum = _sel_dot(ltri, a)
    e_r = lax.broadcasted_iota(jnp.int32, (N_HEADS, N_HEADS), 0)
    e_c = lax.broadcasted_iota(jnp.int32, (N_HEADS, N_HEADS), 1)
    eye = jnp.where(e_r == e_c, 1.0, 0.0).astype(BF16)
    cum_t = _sel_dot_nt(eye, cum)
    last = cum[L - 1:L, :]

    rexp = rexp_ref[...]
    dtx = _dot_sel(dt, rexp)
    wendx = _dot_sel(jnp.exp(last - cum), rexp)
    ecumx = _dot_sel(jnp.exp(cum), rexp)
    xdt = xs * dtx
    xdt_b = xdt.astype(BF16)
    xw_b = (xdt * wendx).astype(BF16)
    elast = jnp.exp(cum_t[:, L - 1:L])
    elast_b = jnp.broadcast_to(elast, (N_HEADS, SSD_STATE))
    rexpt = rexpt_ref[...]
    lane = lax.broadcasted_iota(jnp.int32, (L, LANES), 1)

    heads_per_group = N_HEADS // SSD_GROUPS
    ys = []
    for g in range(SSD_GROUPS):
        gs = slice(GROUP_DIM * g, GROUP_DIM * (g + 1))
        cg = cmat[:, SSD_STATE * g:SSD_STATE * (g + 1)]
        bg = bmat[:, SSD_STATE * g:SSD_STATE * (g + 1)]
        cb = _dot_nt(cg, bg)
        st = state_ref[gs, :]
        yoff = _dot_nt(cg, st.astype(BF16))
        for pr in range(heads_per_group // 2):
            ps = slice(GROUP_DIM * g + LANES * pr, GROUP_DIM * g + LANES * (pr + 1))
            slab = xdt_b[:, ps]
            res = []
            for hh in range(2):
                hd = heads_per_group * g + 2 * pr + hh
                seg = cum[:, hd:hd + 1] - cum_t[hd:hd + 1, :]
                dec = jnp.exp(jnp.where(causal, seg, -jnp.inf))
                res.append(_dot((cb * dec).astype(BF16), slab))
            ydiag = jnp.where(lane < HEAD_DIM, res[0], res[1])
            ys.append(ydiag + yoff[:, LANES * pr:LANES * (pr + 1)] * ecumx[:, ps])
        snew = _dot_tn(xw_b[:, gs], bg)
        scale = _sel_dot(rexpt[gs, :], elast_b)
        state_ref[gs, :] = scale * st + snew

    y = jnp.concatenate(ys, axis=-1) + dskip_ref[...] * xs
    zt = z_ref[0]
    ug = y * (zt * _sigmoid(zt))
    outs = []
    for g in range(SSD_GROUPS):
        ugg = ug[:, GROUP_DIM * g:GROUP_DIM * (g + 1)]
        outs.append(ugg * lax.rsqrt(jnp.mean(ugg * ugg, axis=-1, keepdims=True) + EPS))
    y_ref[0] = (jnp.concatenate(outs, axis=-1) * gssd_ref[...]).astype(BF16)

    @pl.when(c == pl.num_programs(1) - 1)
    def _():
        hfin_ref[0] = state_ref[...]


def _ssd(xbc, z, dt, conv_prev8, ssm_prev, w_conv, b_conv, a_log, d_skip, g_ssd, *, L):
    b, t, _ = xbc.shape
    assert t % L == 0 and L % SUBLANES == 0
    head_of_lane = jnp.arange(SSD_DIM) // HEAD_DIM
    rexp = (jnp.arange(N_HEADS)[:, None] == head_of_lane[None, :]).astype(BF16)
    tile = lambda w: pl.BlockSpec((1, L, w), lambda bi, ci: (bi, ci, 0))
    per_b = lambda s: pl.BlockSpec((1,) + s, lambda bi, ci: (bi, 0, 0))
    return pl.pallas_call(
        functools.partial(_ssd_kernel, L=L),
        grid=(b, t // L),
        in_specs=[tile(CONV_DIM), tile(SSD_DIM), tile(N_HEADS),
                  per_b((SUBLANES, CONV_DIM)), per_b((SSD_DIM, SSD_STATE)),
                  _resident((CONV_W, CONV_DIM)), _resident((1, CONV_DIM)), _resident((1, N_HEADS)),
                  _resident((1, SSD_DIM)), _resident((1, SSD_DIM)),
                  _resident((N_HEADS, SSD_DIM)), _resident((SSD_DIM, N_HEADS))],
        out_specs=(tile(SSD_DIM), per_b((SSD_DIM, SSD_STATE))),
        out_shape=(jax.ShapeDtypeStruct((b, t, SSD_DIM), BF16),
                   jax.ShapeDtypeStruct((b, SSD_DIM, SSD_STATE), F32)),
        scratch_shapes=[pltpu.VMEM((SUBLANES, CONV_DIM), F32),
                        pltpu.VMEM((SSD_DIM, SSD_STATE), F32)],
        compiler_params=_params(("arbitrary", "arbitrary")),
        name="ssd_scan",
    )(xbc, z, dt, conv_prev8, ssm_prev, w_conv, b_conv.reshape(1, CONV_DIM), a_log.reshape(1, N_HEADS),
      jnp.repeat(d_skip, HEAD_DIM).reshape(1, SSD_DIM), g_ssd.reshape(1, SSD_DIM), rexp, rexp.T)


def _ffn_kernel(x_ref, attn_ref, y_ref, wo_ref, g_ref, wg_ref, wu_ref, wd_ref, o_ref, hf_ref):
    j = pl.program_id(1)

    @pl.when(j == 0)
    def _():
        mix = _dot(attn_ref[...], wo_ref[0:FOX_DIM, :]) + _dot(y_ref[...], wo_ref[FOX_DIM:, :])
        x = x_ref[...] + mix
        ms = jnp.mean(x * x, axis=-1, keepdims=True)
        hf_ref[...] = (x * lax.rsqrt(ms + EPS) * g_ref[...]).astype(BF16)
        o_ref[...] = x

    hf = hf_ref[...]
    gate = _dot(hf, wg_ref[...])
    act = (gate * _sigmoid(gate) * _dot(hf, wu_ref[...])).astype(BF16)
    o_ref[...] += _dot(act, wd_ref[...])


def _outproj_ffn(x2d, attn, y, w_out, g_ffn, w_gate, w_up, w_down):
    n, d = x2d.shape
    dff = w_gate.shape[1]
    tm, tf = min(512, n), 512
    assert n % tm == 0 and dff % tf == 0
    row = lambda w: pl.BlockSpec((tm, w), lambda i, j: (i, 0))
    return pl.pallas_call(
        _ffn_kernel,
        grid=(n // tm, dff // tf),
        in_specs=[row(d), row(FOX_DIM), row(SSD_DIM), _resident((FOX_DIM + SSD_DIM, d)),
                  _resident((1, d)),
                  pl.BlockSpec((d, tf), lambda i, j: (0, j)),
                  pl.BlockSpec((d, tf), lambda i, j: (0, j)),
                  pl.BlockSpec((tf, d), lambda i, j: (j, 0))],
        out_specs=pl.BlockSpec((tm, d), lambda i, j: (i, 0)),
        out_shape=jax.ShapeDtypeStruct((n, d), F32),
        scratch_shapes=[pltpu.VMEM((tm, d), BF16)],
        compiler_params=_params(("arbitrary", "arbitrary")),
        name="ffn",
    )(x2d, attn, y, w_out, g_ffn.reshape(1, d), w_gate, w_up, w_down)


def _pack_w_in(w_in):
    d = w_in.shape[0]
    o_f = 3 * FOX_DIM
    o_z = o_f + N_HEADS
    o_xbc = o_z + SSD_DIM
    o_dt = o_xbc + CONV_DIM
    pad = jnp.zeros((d, LANES - N_HEADS), w_in.dtype)
    return jnp.concatenate([w_in[:, :o_f], w_in[:, o_z:o_xbc], w_in[:, o_xbc:o_dt],
                            w_in[:, o_f:o_z], pad, w_in[:, o_dt:], pad], axis=1).astype(BF16)


def _layer(x, conv_prev, ssm_prev, k_prev, v_prev, logf_prev, wts, *, attn_tq, attn_tk, fbias_blk, ssd_chunk):
    (g_mix, w_all, f_bias, g_q, g_k, w_conv, b_conv, dt_bias, a_log, d_skip, g_ssd,
     w_out, g_ffn, w_gate, w_up, w_down) = wts
    b, t, d = x.shape
    n = b * t
    x2d = x.reshape(n, d)
    qb, k, kb, v, vb, z, xbc, logf, dt = _inproj(x2d, g_mix, w_all, g_q, g_k, f_bias, dt_bias)

    past = 0 if k_prev is None else k_prev.shape[1]
    kb3, vb3, logf3 = kb.reshape(b, t, FOX_DIM), vb.reshape(b, t, FOX_DIM), logf.reshape(b, t, N_HEADS)
    logf_all = logf3
    if past:
        kb3 = jnp.concatenate([k_prev.reshape(b, past, FOX_DIM).astype(BF16), kb3], axis=1)
        vb3 = jnp.concatenate([v_prev.reshape(b, past, FOX_DIM).astype(BF16), vb3], axis=1)
        logf_all = jnp.concatenate([logf_prev.astype(F32), logf3], axis=1)
    tkeys = past + t
    tk = attn_tk
    if tkeys % tk:
        tk = -(-tkeys // LANES) * LANES
        grow = ((0, 0), (0, tk - tkeys), (0, 0))
        kb3, vb3, logf_all = jnp.pad(kb3, grow), jnp.pad(vb3, grow), jnp.pad(logf_all, grow)
    kx, qx = _fbias(logf_all, min(fbias_blk, tk))
    attn = _attention(qb.reshape(b, t, FOX_DIM), qx, kb3, kx, jnp.transpose(vb3, (0, 2, 1)),
                      past=past, tq=attn_tq, tk=tk)

    conv_prev8 = jnp.pad(conv_prev.astype(F32), ((0, 0), (SUBLANES - (CONV_W - 1), 0), (0, 0)))
    y, h_last = _ssd(xbc.reshape(b, t, CONV_DIM), z.reshape(b, t, SSD_DIM), dt.reshape(b, t, N_HEADS),
                     conv_prev8, ssm_prev.astype(F32).reshape(b, SSD_DIM, SSD_STATE),
                     w_conv, b_conv, a_log, d_skip, g_ssd, L=ssd_chunk)

    out = _outproj_ffn(x2d, attn.reshape(n, FOX_DIM), y.reshape(n, SSD_DIM), w_out, g_ffn, w_gate, w_up, w_down)

    xbc_all = jnp.concatenate([conv_prev.astype(F32), xbc.reshape(b, t, CONV_DIM)], axis=1)
    return (out.reshape(b, t, d), xbc_all[:, -(CONV_W - 1):],
            h_last.reshape(b, N_HEADS, HEAD_DIM, SSD_STATE),
            k.reshape(b, t, N_HEADS, HEAD_DIM), v.reshape(b, t, N_HEADS, HEAD_DIM), logf3)


def kernel(x_prompt, x_sample, cache_conv, state_ssm, cache_fox_k, cache_fox_v, cache_fox_logf, g_mix, w_in, f_bias, g_q, g_k, w_conv, b_conv, dt_bias, a_log, d_skip, g_ssd, w_out, g_ffn, w_gate, w_up, w_down):
    depth = g_mix.shape[0]
    yp, ys = x_prompt, x_sample
    bp = x_prompt.shape[0]
    outs_p = [[] for _ in range(5)]
    outs_s = [[] for _ in range(5)]
    for i in range(depth):
        wts = (g_mix[i], _pack_w_in(w_in[i]), f_bias[i], g_q[i], g_k[i], w_conv[i], b_conv[i], dt_bias[i],
               a_log[i], d_skip[i], g_ssd[i], w_out[i].astype(BF16), g_ffn[i],
               w_gate[i].astype(BF16), w_up[i].astype(BF16), w_down[i].astype(BF16))
        tp = yp.shape[1]
        yp, *rest = _layer(
            yp, jnp.zeros((bp, CONV_W - 1, CONV_DIM), F32),
            jnp.zeros((bp, N_HEADS, HEAD_DIM, SSD_STATE), F32), None, None, None, wts,
            attn_tq=min(1024, tp), attn_tk=min(1024, tp), fbias_blk=256, ssd_chunk=min(128, tp))
        for lst, val in zip(outs_p, rest):
            lst.append(val)
        tsamp = ys.shape[1]
        ys, *rest = _layer(
            ys, cache_conv[i], state_ssm[i], cache_fox_k[i], cache_fox_v[i], cache_fox_logf[i], wts,
            attn_tq=tsamp, attn_tk=LANES, fbias_blk=LANES, ssd_chunk=tsamp)
        for lst, val in zip(outs_s, rest):
            lst.append(val)
    return (yp, ys, *[jnp.stack(l) for l in outs_p], *[jnp.stack(l) for l in outs_s])
```

```python
import functools

import jax
import jax.numpy as jnp
import numpy as np
from jax import lax
from jax.experimental import pallas as pl
from jax.experimental.pallas import tpu as pltpu

F32 = jnp.float32
BF16 = jnp.bfloat16

EPS = 1e-6
HEAD_DIM = 64
N_HEADS = 16
FOX_DIM = 1024
SSD_DIM = 1024
SSD_GROUPS = 2
SSD_STATE = 128
GROUP_DIM = SSD_DIM // SSD_GROUPS
CONV_W = 4
CONV_DIM = SSD_DIM + 2 * SSD_GROUPS * SSD_STATE
LANES = 128
SUBLANES = 8
MXU_DIM = 256
NEG_BIG = -1e30
LOG2E = 1.4426950408889634
VMEM_LIMIT = 56 * 1024 * 1024

COL_Q, COL_K, COL_V, COL_Z = 0, FOX_DIM, 2 * FOX_DIM, 3 * FOX_DIM
COL_XBC = COL_Z + SSD_DIM
COL_F = COL_XBC + CONV_DIM
COL_DT = COL_F + LANES
IN_COLS = COL_DT + LANES


def _dot(a, b):
    return jnp.dot(a, b, preferred_element_type=F32)


def _dot_nt(a, b):
    return lax.dot_general(a, b, (((1,), (1,)), ((), ())), preferred_element_type=F32)


def _dot_tn(a, b):
    return lax.dot_general(a, b, (((0,), (0,)), ((), ())), preferred_element_type=F32)


def _split3(x):
    x1 = x.astype(BF16)
    r = x - x1.astype(F32)
    x2 = r.astype(BF16)
    r = r - x2.astype(F32)
    return x1, x2, r.astype(BF16)


def _sel_dot(sel, x):
    x1, x2, x3 = _split3(x)
    return _dot(sel, x1) + _dot(sel, x2) + _dot(sel, x3)


def _dot_sel(x, sel):
    x1, x2, x3 = _split3(x)
    return _dot(x1, sel) + _dot(x2, sel) + _dot(x3, sel)


def _sel_dot_nt(sel, x):
    x1, x2, x3 = _split3(x)
    return _dot_nt(sel, x1) + _dot_nt(sel, x2) + _dot_nt(sel, x3)


def _sigmoid(x):
    return 0.5 * jnp.tanh(0.5 * x) + 0.5


def _softplus(x):
    return jnp.maximum(x, 0.0) + jnp.log1p(jnp.exp(-jnp.abs(x)))


def _resident(shape):
    nd = len(shape)
    return pl.BlockSpec(shape, lambda *_: (0,) * nd, pipeline_mode=pl.Buffered(1))


def _params(sem):
    return pltpu.CompilerParams(dimension_semantics=sem, vmem_limit_bytes=VMEM_LIMIT)


def _inproj_kernel(x_ref, gmix_ref, w_ref, e_ref, gq_ref, gk_ref, fb_ref, dtb_ref,
                   q_ref, k_ref, kb_ref, v_ref, vb_ref, z_ref, xbc_ref, logf_ref, dt_ref, *, v_transposed):
    x = x_ref[...]
    ms = jnp.mean(x * x, axis=-1, keepdims=True)
    h = (x * lax.rsqrt(ms + EPS) * gmix_ref[...]).astype(BF16)
    e = e_ref[...]

    def head_rms(y, g):
        ysq = y * y
        hi = ysq.astype(BF16)
        lo = (ysq - hi.astype(F32)).astype(BF16)
        parts = []
        for c in range(FOX_DIM // MXU_DIM):
            sl = slice(MXU_DIM * c, MXU_DIM * (c + 1))
            parts.append(_dot(hi[:, sl], e) + _dot(lo[:, sl], e))
        ss = jnp.concatenate(parts, axis=-1)
        return y * lax.rsqrt(ss * (1.0 / HEAD_DIM) + EPS) * g

    q = _dot(h, w_ref[:, COL_Q:COL_Q + FOX_DIM])
    q_ref[...] = (head_rms(q, gq_ref[...]) * (HEAD_DIM ** -0.5 * LOG2E)).astype(BF16)
    k = head_rms(_dot(h, w_ref[:, COL_K:COL_K + FOX_DIM]), gk_ref[...])
    for hd in range(N_HEADS):
        k_ref[:, hd, :] = k[:, HEAD_DIM * hd:HEAD_DIM * (hd + 1)]
    kb_ref[...] = k.astype(BF16)
    v = _dot(h, w_ref[:, COL_V:COL_V + FOX_DIM])
    for hd in range(N_HEADS):
        v_ref[:, hd, :] = v[:, HEAD_DIM * hd:HEAD_DIM * (hd + 1)]
    if v_transposed:
        vb_ref[0] = v.T.astype(BF16)
    else:
        vb_ref[...] = v.astype(BF16)
    z_ref[...] = _dot(h, w_ref[:, COL_Z:COL_Z + SSD_DIM])
    xbc_ref[...] = _dot(h, w_ref[:, COL_XBC:COL_XBC + CONV_DIM])
    f_raw = _dot(h, w_ref[:, COL_F:COL_F + LANES])[:, :N_HEADS]
    logf_ref[...] = -_softplus(-(f_raw + fb_ref[...]))
    dt_raw = _dot(h, w_ref[:, COL_DT:COL_DT + LANES])[:, :N_HEADS]
    dt_ref[...] = _softplus(dt_raw + dtb_ref[...])


def _inproj(x2d, seq, g_mix, w_all, g_q, g_k, f_bias, dt_bias):
    n, d = x2d.shape
    tm = min(256, n)
    assert n % tm == 0 and n % seq == 0
    v_transposed = seq % tm == 0
    if v_transposed:
        tiles = seq // tm
        vb_shape = jax.ShapeDtypeStruct((n // seq, FOX_DIM, seq), BF16)
        vb_spec = pl.BlockSpec((1, FOX_DIM, tm), lambda i: (i // tiles, 0, i % tiles))
    else:
        vb_shape = jax.ShapeDtypeStruct((n, FOX_DIM), BF16)
        vb_spec = pl.BlockSpec((tm, FOX_DIM), lambda i: (i, 0))
    blk = jnp.arange(MXU_DIM) // HEAD_DIM
    e = (blk[:, None] == blk[None, :]).astype(BF16)
    row = lambda w: pl.BlockSpec((tm, w), lambda i: (i, 0))
    heads = pl.BlockSpec((tm, N_HEADS, HEAD_DIM), lambda i: (i, 0, 0))
    out_shapes = (
        jax.ShapeDtypeStruct((n, FOX_DIM), BF16),
        jax.ShapeDtypeStruct((n, N_HEADS, HEAD_DIM), F32),
        jax.ShapeDtypeStruct((n, FOX_DIM), BF16),
        jax.ShapeDtypeStruct((n, N_HEADS, HEAD_DIM), F32),
        vb_shape,
        jax.ShapeDtypeStruct((n, SSD_DIM), F32),
        jax.ShapeDtypeStruct((n, CONV_DIM), F32),
        jax.ShapeDtypeStruct((n, N_HEADS), F32),
        jax.ShapeDtypeStruct((n, N_HEADS), F32),
    )
    return pl.pallas_call(
        functools.partial(_inproj_kernel, v_transposed=v_transposed),
        grid=(n // tm,),
        in_specs=[row(d), _resident((1, d)), _resident((d, IN_COLS)), _resident((MXU_DIM, MXU_DIM)),
                  _resident((1, FOX_DIM)), _resident((1, FOX_DIM)),
                  _resident((1, N_HEADS)), _resident((1, N_HEADS))],
        out_specs=(row(FOX_DIM), heads, row(FOX_DIM), heads, vb_spec,
                   row(SSD_DIM), row(CONV_DIM), row(N_HEADS), row(N_HEADS)),
        out_shape=out_shapes,
        compiler_params=_params(("arbitrary",)),
        name="inproj",
    )(x2d, g_mix.reshape(1, d), w_all, e,
      jnp.tile(g_q, N_HEADS).reshape(1, FOX_DIM), jnp.tile(g_k, N_HEADS).reshape(1, FOX_DIM),
      f_bias.reshape(1, N_HEADS), dt_bias.reshape(1, N_HEADS))


N_EXTRA = 3
Q_EXTRA_SPLIT = 2 * N_EXTRA
PAIR_EXTRA = 16


def _placement_constants():
    pq = np.zeros((N_EXTRA, N_HEADS, LANES), np.float32)
    pk = np.zeros((N_EXTRA, N_HEADS, LANES), np.float32)
    oq = np.zeros((1, LANES), np.float32)
    ok = np.zeros((1, LANES), np.float32)
    for h in range(N_HEADS):
        base = PAIR_EXTRA * (h // 2)
        for c in range(N_EXTRA):
            if h % 2 == 0:
                pq[c, h, base + c] = 1.0
                pk[c, h, base + N_EXTRA + c] = -1.0
                oq[0, base + N_EXTRA + c] = 1.0
                ok[0, base + c] = 1.0
            else:
                pq[c, h, base + 3 * N_EXTRA + c] = 1.0
                pk[c, h, base + 2 * N_EXTRA + c] = -1.0
                oq[0, base + 2 * N_EXTRA + c] = 1.0
                ok[0, base + 3 * N_EXTRA + c] = 1.0
    return (jnp.asarray(pq, BF16), jnp.asarray(pk, BF16), jnp.asarray(oq), jnp.asarray(ok))


def _fbias_kernel(logf_ref, pq_ref, pk_ref, oq_ref, ok_ref, kx_ref, qx_ref, carry_ref, *, blk, nblk):
    r = lax.broadcasted_iota(jnp.int32, (blk, blk), 0)
    c = lax.broadcasted_iota(jnp.int32, (blk, blk), 1)
    ltri = jnp.where(r >= c, 1.0, 0.0).astype(BF16)

    @pl.when(pl.program_id(1) == 0)
    def _():
        carry_ref[...] = jnp.zeros(carry_ref.shape, F32)

    lane = lax.broadcasted_iota(jnp.int32, (blk, N_EXTRA * N_HEADS), 1)

    def body(i, carry):
        off = pl.multiple_of(i * blk, blk)
        f = _sel_dot(ltri, logf_ref[0, pl.ds(off, blk), :]) + carry
        p1, p2, p3 = _split3(f * LOG2E)
        pieces = jnp.where(lane < N_HEADS, p1, jnp.where(lane < 2 * N_HEADS, p2, p3))
        kx_ref[0, pl.ds(off, blk), :] = (ok_ref[...] + _dot(pieces, pk_ref[...])).astype(BF16)
        qx_ref[0, pl.ds(off, blk), :] = (oq_ref[...] + _dot(pieces, pq_ref[...])).astype(BF16)
        return f[blk - 1:blk, :]

    carry_ref[...] = lax.fori_loop(0, nblk, body, carry_ref[...], unroll=True)


def _fbias(logf3, blk):
    b, t, _ = logf3.shape
    tt = t if t <= 4096 else 2048
    assert t % tt == 0 and tt % blk == 0
    pq, pk, oq, ok = _placement_constants()
    width = N_EXTRA * N_HEADS
    out = jax.ShapeDtypeStruct((b, t, LANES), BF16)
    tile = lambda w: pl.BlockSpec((1, tt, w), lambda bi, ti: (bi, ti, 0))
    return pl.pallas_call(
        functools.partial(_fbias_kernel, blk=blk, nblk=tt // blk),
        grid=(b, t // tt),
        in_specs=[tile(width), _resident((width, LANES)), _resident((width, LANES)),
                  _resident((1, LANES)), _resident((1, LANES))],
        out_specs=(tile(LANES), tile(LANES)),
        out_shape=(out, out),
        scratch_shapes=[pltpu.VMEM((1, width), F32)],
        compiler_params=_params(("arbitrary", "arbitrary")),
        name="fbias",
    )(jnp.tile(logf3, (1, 1, N_EXTRA)), pq.reshape(width, LANES), pk.reshape(width, LANES), oq, ok)


def _pair_selectors():
    sel = np.zeros((N_HEADS // 2, LANES, LANES), np.float32)
    for p in range(N_HEADS // 2):
        for e in range(PAIR_EXTRA):
            sel[p, PAIR_EXTRA * p + e, e] = 1.0
    return jnp.asarray(sel, BF16)


def _attn_kernel(q_ref, qx_ref, k_ref, kx_ref, sel_ref, vt_ref, o_ref, kcat_ref, qs_ref, acc_ref, sa_ref, sb_ref,
                 *, tq, tsub, tk, past):
    i = pl.program_id(2)
    nsub = tq // tsub
    chains = [(sb, a) for sb in range(nsub) for a in range(2)]
    sel = sel_ref[0]

    @pl.when(i == 0)
    def _():
        kcat_ref[:, 0:LANES] = k_ref[0]

        def fill(r, carry):
            off = pl.multiple_of(r * tk, tk)
            kcat_ref[pl.ds(off, tk), LANES:2 * LANES] = _dot(kx_ref[0, pl.ds(off, tk), :], sel).astype(BF16)
            return carry

        lax.fori_loop(0, kcat_ref.shape[0] // tk, fill, 0)

    lane = lax.broadcasted_iota(jnp.int32, (tsub, LANES), 1)
    for sb in range(nsub):
        q2 = q_ref[0, tsub * sb:tsub * (sb + 1), :].astype(F32)
        qx2 = _dot(qx_ref[0, tsub * sb:tsub * (sb + 1), :], sel)
        zero = jnp.zeros_like(q2)
        for a, keep_q, keep_x in ((0, lane < HEAD_DIM, lane < Q_EXTRA_SPLIT),
                                  (1, lane >= HEAD_DIM, lane >= Q_EXTRA_SPLIT)):
            qs_ref[2 * sb + a, 0:LANES, :] = jnp.where(keep_q, q2, zero).T.astype(BF16)
            qs_ref[2 * sb + a, LANES:2 * LANES, :] = jnp.where(keep_x, qx2, zero).T.astype(BF16)
    acc_ref[...] = jnp.zeros(acc_ref.shape, F32)

    q_lo = past + i * tq
    jm = q_lo // tk

    def qk_stage(j, s_ref, masked):
        off = pl.multiple_of(j * tk, tk)
        kc = kcat_ref[pl.ds(off, tk), :]
        mx = []
        for c, (sb, a) in enumerate(chains):
            s = _dot(kc, qs_ref[c])
            if masked:
                kpos = off + lax.broadcasted_iota(jnp.int32, (tk, tsub), 0)
                qpos = q_lo + tsub * sb + lax.broadcasted_iota(jnp.int32, (tk, tsub), 1)
                s = jnp.where(kpos <= qpos, s, NEG_BIG)
            s_ref[c] = s
            mx.append(jnp.max(s, axis=0, keepdims=True))
        return tuple(mx)

    def sm_stage(j, s_ref, mx, state):
        off = pl.multiple_of(j * tk, tk)
        vts = [vt_ref[0, HEAD_DIM * a:HEAD_DIM * (a + 1), pl.ds(off, tk)] for a in range(2)]
        out, probs = [], []
        for c, (sb, a) in enumerate(chains):
            m_prev, l_prev = state[c]
            m_new = jnp.maximum(m_prev, mx[c])
            alpha = jnp.exp2(m_prev - m_new)
            p = jnp.exp2(s_ref[c] - m_new)
            out.append((m_new, alpha * l_prev + jnp.sum(p, axis=0, keepdims=True)))
            probs.append((alpha, p.astype(BF16)))
        for c, (sb, a) in enumerate(chains):
            alpha, p = probs[c]
            acc_ref[c] = alpha * acc_ref[c] + _dot(vts[a], p)
        return tuple(out)

    init = tuple((jnp.full((1, tsub), NEG_BIG, F32), jnp.zeros((1, tsub), F32)) for _ in chains)
    mx0 = qk_stage(jm, sa_ref, True)

    def pair_body(u, carry):
        jprev, mx_a, state = carry
        mx_b = qk_stage(2 * u, sb_ref, False)
        state = sm_stage(jprev, sa_ref, mx_a, state)
        mx_a = qk_stage(2 * u + 1, sa_ref, False)
        state = sm_stage(2 * u, sb_ref, mx_b, state)
        return 2 * u + 1, mx_a, state

    jprev, mx_a, state = lax.fori_loop(0, jm // 2, pair_body, (jm, mx0, init))

    def odd_tail(args):
        jprev, mx_a, state = args
        mx_b = qk_stage(jm - 1, sb_ref, False)
        state = sm_stage(jprev, sa_ref, mx_a, state)
        return sm_stage(jm - 1, sb_ref, mx_b, state)

    def even_tail(args):
        jprev, mx_a, state = args
        return sm_stage(jprev, sa_ref, mx_a, state)

    state = lax.cond(jm % 2 == 1, odd_tail, even_tail, (jprev, mx_a, state))

    for sb in range(nsub):
        ot = jnp.concatenate([acc_ref[2 * sb + a] * (1.0 / state[2 * sb + a][1]) for a in range(2)], axis=0)
        o_ref[0, tsub * sb:tsub * (sb + 1), :] = ot.T.astype(BF16)


def _attention(q, qx, k, kx, vt, *, past, tq, tk):
    b, t, _ = q.shape
    tkeys = k.shape[1]
    npairs = N_HEADS // 2
    assert t % tq == 0 and tkeys % tk == 0 and past % tq == 0 and tk % tq == 0
    tsub = min(tq, MXU_DIM)
    assert tq % tsub == 0
    nchain = 2 * (tq // tsub)
    qoff = past // tq
    nbuf = pl.Buffered(1 if t // tq > 1 else 2)
    return pl.pallas_call(
        functools.partial(_attn_kernel, tq=tq, tsub=tsub, tk=tk, past=past),
        grid=(b, npairs, t // tq),
        in_specs=[pl.BlockSpec((1, tq, LANES), lambda bi, p, i: (bi, i, p)),
                  pl.BlockSpec((1, tq, LANES), lambda bi, p, i: (bi, qoff + i, 0)),
                  pl.BlockSpec((1, tkeys, LANES), lambda bi, p, i: (bi, 0, p), pipeline_mode=nbuf),
                  pl.BlockSpec((1, tkeys, LANES), lambda bi, p, i: (bi, 0, 0), pipeline_mode=nbuf),
                  pl.BlockSpec((1, LANES, LANES), lambda bi, p, i: (p, 0, 0)),
                  pl.BlockSpec((1, LANES, tkeys), lambda bi, p, i: (bi, p, 0))],
        out_specs=pl.BlockSpec((1, tq, LANES), lambda bi, p, i: (bi, i, p)),
        out_shape=jax.ShapeDtypeStruct((b, t, FOX_DIM), BF16),
        scratch_shapes=[pltpu.VMEM((tkeys, 2 * LANES), BF16),
                        pltpu.VMEM((nchain, 2 * LANES, tsub), BF16),
                        pltpu.VMEM((nchain, HEAD_DIM, tsub), F32),
                        pltpu.VMEM((nchain, tk, tsub), F32),
                        pltpu.VMEM((nchain, tk, tsub), F32)],
        compiler_params=_params(("arbitrary", "arbitrary", "arbitrary")),
        name="fox_attention",
    )(q, qx, k, kx, _pair_selectors(), vt)


def _ssd_kernel(xbc_ref, z_ref, dt_ref, cprev_ref, sprev_ref, wconv_ref, bconv_ref, alog_ref,
                dskip_ref, gssd_ref, rexp_ref, rexpt_ref,
                y_ref, hfin_ref, ext_ref, state_ref, *, L):
    c = pl.program_id(1)

    @pl.when(c == 0)
    def _():
        state_ref[...] = sprev_ref[0]
        ext_ref[...] = cprev_ref[0]

    x_tile = xbc_ref[0]
    ext = jnp.concatenate([ext_ref[...], x_tile], axis=0)
    w = wconv_ref[...]
    pre = bconv_ref[...] + x_tile * w[CONV_W - 1:CONV_W, :]
    for back in range(1, CONV_W):
        shifted = pltpu.roll(ext, back, axis=0)[SUBLANES:, :]
        pre = pre + shifted * w[CONV_W - 1 - back:CONV_W - back, :]
    ext_ref[...] = x_tile[L - SUBLANES:L, :]
    u = pre * _sigmoid(pre)
    xs = u[:, 0:SSD_DIM]
    bmat = u[:, SSD_DIM:SSD_DIM + SSD_GROUPS * SSD_STATE].astype(BF16)
    cmat = u[:, SSD_DIM + SSD_GROUPS * SSD_STATE:].astype(BF16)

    dt = dt_ref[0]
    a = dt * (-jnp.exp(alog_ref[...]))
    r = lax.broadcasted_iota(jnp.int32, (L, L), 0)
    cc = lax.broadcasted_iota(jnp.int32, (L, L), 1)
    causal = r >= cc
    ltri = jnp.where(causal, 1.0, 0.0).astype(BF16)
    cum = _sel_dot(ltri, a)
    e_r = lax.broadcasted_iota(jnp.int32, (N_HEADS, N_HEADS), 0)
    e_c = lax.broadcasted_iota(jnp.int32, (N_HEADS, N_HEADS), 1)
    eye = jnp.where(e_r == e_c, 1.0, 0.0).astype(BF16)
    cum_t = _sel_dot_nt(eye, cum)
    last = cum[L - 1:L, :]

    rexp = rexp_ref[...]
    dtx = _dot_sel(dt, rexp)
    wendx = _dot_sel(jnp.exp(last - cum), rexp)
    ecumx = _dot_sel(jnp.exp(cum), rexp)
    xdt = xs * dtx
    xdt_b = xdt.astype(BF16)
    xw_b = (xdt * wendx).astype(BF16)
    elast = jnp.exp(cum_t[:, L - 1:L])
    elast_b = jnp.broadcast_to(elast, (N_HEADS, SSD_STATE))
    rexpt = rexpt_ref[...]
    lane = lax.broadcasted_iota(jnp.int32, (L, LANES), 1)

    heads_per_group = N_HEADS // SSD_GROUPS
    ys = []
    for g in range(SSD_GROUPS):
        gs = slice(GROUP_DIM * g, GROUP_DIM * (g + 1))
        cg = cmat[:, SSD_STATE * g:SSD_STATE * (g + 1)]
        bg = bmat[:, SSD_STATE * g:SSD_STATE * (g + 1)]
        cb = _dot_nt(cg, bg)
        st = state_ref[gs, :]
        yoff = _dot_nt(cg, st.astype(BF16))
        for pr in range(heads_per_group // 2):
            ps = slice(GROUP_DIM * g + LANES * pr, GROUP_DIM * g + LANES * (pr + 1))
            slab = xdt_b[:, ps]
            res = []
            for hh in range(2):
                hd = heads_per_group * g + 2 * pr + hh
                seg = cum[:, hd:hd + 1] - cum_t[hd:hd + 1, :]
                dec = jnp.exp(jnp.where(causal, seg, -jnp.inf))
                res.append(_dot((cb * dec).astype(BF16), slab))
            ydiag = jnp.where(lane < HEAD_DIM, res[0], res[1])
            ys.append(ydiag + yoff[:, LANES * pr:LANES * (pr + 1)] * ecumx[:, ps])
        snew = _dot_tn(xw_b[:, gs], bg)
        scale = _sel_dot(rexpt[gs, :], elast_b)
        state_ref[gs, :] = scale * st + snew

    y = jnp.concatenate(ys, axis=-1) + dskip_ref[...] * xs
    zt = z_ref[0]
    ug = y * (zt * _sigmoid(zt))
    outs = []
    for g in range(SSD_GROUPS):
        ugg = ug[:, GROUP_DIM * g:GROUP_DIM * (g + 1)]
        outs.append(ugg * lax.rsqrt(jnp.mean(ugg * ugg, axis=-1, keepdims=True) + EPS))
    y_ref[0] = (jnp.concatenate(outs, axis=-1) * gssd_ref[...]).astype(BF16)

    @pl.when(c == pl.num_programs(1) - 1)
    def _():
        hfin_ref[0] = state_ref[...]


def _ssd(xbc, z, dt, conv_prev8, ssm_prev, w_conv, b_conv, a_log, d_skip, g_ssd, *, L):
    b, t, _ = xbc.shape
    assert t % L == 0 and L % SUBLANES == 0
    head_of_lane = jnp.arange(SSD_DIM) // HEAD_DIM
    rexp = (jnp.arange(N_HEADS)[:, None] == head_of_lane[None, :]).astype(BF16)
    tile = lambda w: pl.BlockSpec((1, L, w), lambda bi, ci: (bi, ci, 0))
    per_b = lambda s: pl.BlockSpec((1,) + s, lambda bi, ci: (bi, 0, 0))
    return pl.pallas_call(
        functools.partial(_ssd_kernel, L=L),
        grid=(b, t // L),
        in_specs=[tile(CONV_DIM), tile(SSD_DIM), tile(N_HEADS),
                  per_b((SUBLANES, CONV_DIM)), per_b((SSD_DIM, SSD_STATE)),
                  _resident((CONV_W, CONV_DIM)), _resident((1, CONV_DIM)), _resident((1, N_HEADS)),
                  _resident((1, SSD_DIM)), _resident((1, SSD_DIM)),
                  _resident((N_HEADS, SSD_DIM)), _resident((SSD_DIM, N_HEADS))],
        out_specs=(tile(SSD_DIM), per_b((SSD_DIM, SSD_STATE))),
        out_shape=(jax.ShapeDtypeStruct((b, t, SSD_DIM), BF16),
                   jax.ShapeDtypeStruct((b, SSD_DIM, SSD_STATE), F32)),
        scratch_shapes=[pltpu.VMEM((SUBLANES, CONV_DIM), F32),
                        pltpu.VMEM((SSD_DIM, SSD_STATE), F32)],
        compiler_params=_params(("arbitrary", "arbitrary")),
        name="ssd_scan",
    )(xbc, z, dt, conv_prev8, ssm_prev, w_conv, b_conv.reshape(1, CONV_DIM), a_log.reshape(1, N_HEADS),
      jnp.repeat(d_skip, HEAD_DIM).reshape(1, SSD_DIM), g_ssd.reshape(1, SSD_DIM), rexp, rexp.T)


def _ffn_kernel(x_ref, attn_ref, y_ref, wo_ref, g_ref, wg_ref, wu_ref, wd_ref, o_ref, hf_ref):
    j = pl.program_id(1)

    @pl.when(j == 0)
    def _():
        mix = _dot(attn_ref[...], wo_ref[0:FOX_DIM, :]) + _dot(y_ref[...], wo_ref[FOX_DIM:, :])
        x = x_ref[...] + mix
        ms = jnp.mean(x * x, axis=-1, keepdims=True)
        hf_ref[...] = (x * lax.rsqrt(ms + EPS) * g_ref[...]).astype(BF16)
        o_ref[...] = x

    hf = hf_ref[...]
    gate = _dot(hf, wg_ref[...])
    act = (gate * _sigmoid(gate) * _dot(hf, wu_ref[...])).astype(BF16)
    o_ref[...] += _dot(act, wd_ref[...])


def _outproj_ffn(x2d, attn, y, w_out, g_ffn, w_gate, w_up, w_down):
    n, d = x2d.shape
    dff = w_gate.shape[1]
    tm, tf = min(512, n), 512
    assert n % tm == 0 and dff % tf == 0
    row = lambda w: pl.BlockSpec((tm, w), lambda i, j: (i, 0))
    return pl.pallas_call(
        _ffn_kernel,
        grid=(n // tm, dff // tf),
        in_specs=[row(d), row(FOX_DIM), row(SSD_DIM), _resident((FOX_DIM + SSD_DIM, d)),
                  _resident((1, d)),
                  pl.BlockSpec((d, tf), lambda i, j: (0, j)),
                  pl.BlockSpec((d, tf), lambda i, j: (0, j)),
                  pl.BlockSpec((tf, d), lambda i, j: (j, 0))],
        out_specs=pl.BlockSpec((tm, d), lambda i, j: (i, 0)),
        out_shape=jax.ShapeDtypeStruct((n, d), F32),
        scratch_shapes=[pltpu.VMEM((tm, d), BF16)],
        compiler_params=_params(("arbitrary", "arbitrary")),
        name="ffn",
    )(x2d, attn, y, w_out, g_ffn.reshape(1, d), w_gate, w_up, w_down)


def _pack_w_in(w_in):
    d = w_in.shape[0]
    o_f = 3 * FOX_DIM
    o_z = o_f + N_HEADS
    o_xbc = o_z + SSD_DIM
    o_dt = o_xbc + CONV_DIM
    pad = jnp.zeros((d, LANES - N_HEADS), w_in.dtype)
    return jnp.concatenate([w_in[:, :o_f], w_in[:, o_z:o_xbc], w_in[:, o_xbc:o_dt],
                            w_in[:, o_f:o_z], pad, w_in[:, o_dt:], pad], axis=1).astype(BF16)


def _layer(x, conv_prev, ssm_prev, k_prev, v_prev, logf_prev, wts, *, attn_tq, attn_tk, fbias_blk, ssd_chunk):
    (g_mix, w_all, f_bias, g_q, g_k, w_conv, b_conv, dt_bias, a_log, d_skip, g_ssd,
     w_out, g_ffn, w_gate, w_up, w_down) = wts
    b, t, d = x.shape
    n = b * t
    x2d = x.reshape(n, d)
    qb, k, kb, v, vb, z, xbc, logf, dt = _inproj(x2d, t, g_mix, w_all, g_q, g_k, f_bias, dt_bias)

    past = 0 if k_prev is None else k_prev.shape[1]
    kb3, logf3 = kb.reshape(b, t, FOX_DIM), logf.reshape(b, t, N_HEADS)
    vt3 = vb if vb.ndim == 3 else jnp.transpose(vb.reshape(b, t, FOX_DIM), (0, 2, 1))
    logf_all = logf3
    if past:
        kb3 = jnp.concatenate([k_prev.reshape(b, past, FOX_DIM).astype(BF16), kb3], axis=1)
        vt_prev = jnp.transpose(v_prev.reshape(b, past, FOX_DIM).astype(BF16), (0, 2, 1))
        vt3 = jnp.concatenate([vt_prev, vt3], axis=2)
        logf_all = jnp.concatenate([logf_prev.astype(F32), logf3], axis=1)
    tkeys = past + t
    tk = attn_tk
    if tkeys % tk:
        tk = -(-tkeys // LANES) * LANES
        grow = ((0, 0), (0, tk - tkeys), (0, 0))
        kb3, logf_all = jnp.pad(kb3, grow), jnp.pad(logf_all, grow)
        vt3 = jnp.pad(vt3, ((0, 0), (0, 0), (0, tk - tkeys)))
    kx, qx = _fbias(logf_all, min(fbias_blk, tk))
    attn = _attention(qb.reshape(b, t, FOX_DIM), qx, kb3, kx, vt3, past=past, tq=attn_tq, tk=tk)

    conv_prev8 = jnp.pad(conv_prev.astype(F32), ((0, 0), (SUBLANES - (CONV_W - 1), 0), (0, 0)))
    y, h_last = _ssd(xbc.reshape(b, t, CONV_DIM), z.reshape(b, t, SSD_DIM), dt.reshape(b, t, N_HEADS),
                     conv_prev8, ssm_prev.astype(F32).reshape(b, SSD_DIM, SSD_STATE),
                     w_conv, b_conv, a_log, d_skip, g_ssd, L=ssd_chunk)

    out = _outproj_ffn(x2d, attn.reshape(n, FOX_DIM), y.reshape(n, SSD_DIM), w_out, g_ffn, w_gate, w_up, w_down)

    xbc_all = jnp.concatenate([conv_prev.astype(F32), xbc.reshape(b, t, CONV_DIM)], axis=1)
    return (out.reshape(b, t, d), xbc_all[:, -(CONV_W - 1):],
            h_last.reshape(b, N_HEADS, HEAD_DIM, SSD_STATE),
            k.reshape(b, t, N_HEADS, HEAD_DIM), v.reshape(b, t, N_HEADS, HEAD_DIM), logf3)


def kernel(x_prompt, x_sample, cache_conv, state_ssm, cache_fox_k, cache_fox_v, cache_fox_logf, g_mix, w_in, f_bias, g_q, g_k, w_conv, b_conv, dt_bias, a_log, d_skip, g_ssd, w_out, g_ffn, w_gate, w_up, w_down):
    depth = g_mix.shape[0]
    yp, ys = x_prompt, x_sample
    bp = x_prompt.shape[0]
    outs_p = [[] for _ in range(5)]
    outs_s = [[] for _ in range(5)]
    for i in range(depth):
        wts = (g_mix[i], _pack_w_in(w_in[i]), f_bias[i], g_q[i], g_k[i], w_conv[i], b_conv[i], dt_bias[i],
               a_log[i], d_skip[i], g_ssd[i], w_out[i].astype(BF16), g_ffn[i],
               w_gate[i].astype(BF16), w_up[i].astype(BF16), w_down[i].astype(BF16))
        tp = yp.shape[1]
        yp, *rest = _layer(
            yp, jnp.zeros((bp, CONV_W - 1, CONV_DIM), F32),
            jnp.zeros((bp, N_HEADS, HEAD_DIM, SSD_STATE), F32), None, None, None, wts,
            attn_tq=min(1024, tp), attn_tk=min(1024, tp), fbias_blk=256, ssd_chunk=min(128, tp))
        for lst, val in zip(outs_p, rest):
            lst.append(val)
        tsamp = ys.shape[1]
        ys, *rest = _layer(
            ys, cache_conv[i], state_ssm[i], cache_fox_k[i], cache_fox_v[i], cache_fox_logf[i], wts,
            attn_tq=tsamp, attn_tk=LANES, fbias_blk=LANES, ssd_chunk=tsamp)
        for lst, val in zip(outs_s, rest):
            lst.append(val)
    return (yp, ys, *[jnp.stack(l) for l in outs_p], *[jnp.stack(l) for l in outs_s])
```

```python
import functools

import jax
import jax.numpy as jnp
import numpy as np
from jax import lax
from jax.experimental import pallas as pl
from jax.experimental.pallas import tpu as pltpu

F32 = jnp.float32
BF16 = jnp.bfloat16

EPS = 1e-6
HEAD_DIM = 64
N_HEADS = 16
FOX_DIM = 1024
SSD_DIM = 1024
SSD_GROUPS = 2
SSD_STATE = 128
GROUP_DIM = SSD_DIM // SSD_GROUPS
CONV_W = 4
CONV_DIM = SSD_DIM + 2 * SSD_GROUPS * SSD_STATE
LANES = 128
SUBLANES = 8
MXU_DIM = 256
NEG_BIG = -1e30
LOG2E = 1.4426950408889634
VMEM_LIMIT = 56 * 1024 * 1024

COL_Q, COL_K, COL_V, COL_Z = 0, FOX_DIM, 2 * FOX_DIM, 3 * FOX_DIM
COL_XBC = COL_Z + SSD_DIM
COL_F = COL_XBC + CONV_DIM
COL_DT = COL_F + LANES
IN_COLS = COL_DT + LANES


def _dot(a, b):
    return jnp.dot(a, b, preferred_element_type=F32)


def _dot_nt(a, b):
    return lax.dot_general(a, b, (((1,), (1,)), ((), ())), preferred_element_type=F32)


def _dot_tn(a, b):
    return lax.dot_general(a, b, (((0,), (0,)), ((), ())), preferred_element_type=F32)


def _split3(x):
    x1 = x.astype(BF16)
    r = x - x1.astype(F32)
    x2 = r.astype(BF16)
    r = r - x2.astype(F32)
    return x1, x2, r.astype(BF16)


def _sel_dot(sel, x):
    x1, x2, x3 = _split3(x)
    return _dot(sel, x1) + _dot(sel, x2) + _dot(sel, x3)


def _dot_sel(x, sel):
    x1, x2, x3 = _split3(x)
    return _dot(x1, sel) + _dot(x2, sel) + _dot(x3, sel)


def _sel_dot_nt(sel, x):
    x1, x2, x3 = _split3(x)
    return _dot_nt(sel, x1) + _dot_nt(sel, x2) + _dot_nt(sel, x3)


def _sigmoid(x):
    return 0.5 * jnp.tanh(0.5 * x) + 0.5


def _softplus(x):
    return jnp.maximum(x, 0.0) + jnp.log1p(jnp.exp(-jnp.abs(x)))


def _resident(shape):
    nd = len(shape)
    return pl.BlockSpec(shape, lambda *_: (0,) * nd, pipeline_mode=pl.Buffered(1))


def _params(sem):
    return pltpu.CompilerParams(dimension_semantics=sem, vmem_limit_bytes=VMEM_LIMIT)


def _inproj_kernel(x_ref, gmix_ref, w_ref, e_ref, gq_ref, gk_ref, fb_ref, dtb_ref,
                   q_ref, k_ref, kb_ref, v_ref, vb_ref, z_ref, xbc_ref, logf_ref, dt_ref, *, v_transposed):
    x = x_ref[...]
    ms = jnp.mean(x * x, axis=-1, keepdims=True)
    h = (x * lax.rsqrt(ms + EPS) * gmix_ref[...]).astype(BF16)
    e = e_ref[...]

    def head_rms(y, g):
        ysq = y * y
        hi = ysq.astype(BF16)
        lo = (ysq - hi.astype(F32)).astype(BF16)
        parts = []
        for c in range(FOX_DIM // MXU_DIM):
            sl = slice(MXU_DIM * c, MXU_DIM * (c + 1))
            parts.append(_dot(hi[:, sl], e) + _dot(lo[:, sl], e))
        ss = jnp.concatenate(parts, axis=-1)
        return y * lax.rsqrt(ss * (1.0 / HEAD_DIM) + EPS) * g

    q = _dot(h, w_ref[:, COL_Q:COL_Q + FOX_DIM])
    q_ref[...] = (head_rms(q, gq_ref[...]) * (HEAD_DIM ** -0.5 * LOG2E)).astype(BF16)
    k = head_rms(_dot(h, w_ref[:, COL_K:COL_K + FOX_DIM]), gk_ref[...])
    for hd in range(N_HEADS):
        k_ref[:, hd, :] = k[:, HEAD_DIM * hd:HEAD_DIM * (hd + 1)]
    kb_ref[...] = k.astype(BF16)
    v = _dot(h, w_ref[:, COL_V:COL_V + FOX_DIM])
    for hd in range(N_HEADS):
        v_ref[:, hd, :] = v[:, HEAD_DIM * hd:HEAD_DIM * (hd + 1)]
    if v_transposed:
        vb_ref[0] = v.T.astype(BF16)
    else:
        vb_ref[...] = v.astype(BF16)
    z_ref[...] = _dot(h, w_ref[:, COL_Z:COL_Z + SSD_DIM])
    xbc_ref[...] = _dot(h, w_ref[:, COL_XBC:COL_XBC + CONV_DIM])
    f_raw = _dot(h, w_ref[:, COL_F:COL_F + LANES])[:, :N_HEADS]
    logf_ref[...] = -_softplus(-(f_raw + fb_ref[...]))
    dt_raw = _dot(h, w_ref[:, COL_DT:COL_DT + LANES])[:, :N_HEADS]
    dt_ref[...] = _softplus(dt_raw + dtb_ref[...])


def _inproj(x2d, seq, g_mix, w_all, g_q, g_k, f_bias, dt_bias):
    n, d = x2d.shape
    tm = min(256, n)
    assert n % tm == 0 and n % seq == 0
    v_transposed = seq % tm == 0
    if v_transposed:
        tiles = seq // tm
        vb_shape = jax.ShapeDtypeStruct((n // seq, FOX_DIM, seq), BF16)
        vb_spec = pl.BlockSpec((1, FOX_DIM, tm), lambda i: (i // tiles, 0, i % tiles))
    else:
        vb_shape = jax.ShapeDtypeStruct((n, FOX_DIM), BF16)
        vb_spec = pl.BlockSpec((tm, FOX_DIM), lambda i: (i, 0))
    blk = jnp.arange(MXU_DIM) // HEAD_DIM
    e = (blk[:, None] == blk[None, :]).astype(BF16)
    row = lambda w: pl.BlockSpec((tm, w), lambda i: (i, 0))
    heads = pl.BlockSpec((tm, N_HEADS, HEAD_DIM), lambda i: (i, 0, 0))
    out_shapes = (
        jax.ShapeDtypeStruct((n, FOX_DIM), BF16),
        jax.ShapeDtypeStruct((n, N_HEADS, HEAD_DIM), F32),
        jax.ShapeDtypeStruct((n, FOX_DIM), BF16),
        jax.ShapeDtypeStruct((n, N_HEADS, HEAD_DIM), F32),
        vb_shape,
        jax.ShapeDtypeStruct((n, SSD_DIM), F32),
        jax.ShapeDtypeStruct((n, CONV_DIM), F32),
        jax.ShapeDtypeStruct((n, N_HEADS), F32),
        jax.ShapeDtypeStruct((n, N_HEADS), F32),
    )
    return pl.pallas_call(
        functools.partial(_inproj_kernel, v_transposed=v_transposed),
        grid=(n // tm,),
        in_specs=[row(d), _resident((1, d)), _resident((d, IN_COLS)), _resident((MXU_DIM, MXU_DIM)),
                  _resident((1, FOX_DIM)), _resident((1, FOX_DIM)),
                  _resident((1, N_HEADS)), _resident((1, N_HEADS))],
        out_specs=(row(FOX_DIM), heads, row(FOX_DIM), heads, vb_spec,
                   row(SSD_DIM), row(CONV_DIM), row(N_HEADS), row(N_HEADS)),
        out_shape=out_shapes,
        compiler_params=_params(("arbitrary",)),
        name="inproj",
    )(x2d, g_mix.reshape(1, d), w_all, e,
      jnp.tile(g_q, N_HEADS).reshape(1, FOX_DIM), jnp.tile(g_k, N_HEADS).reshape(1, FOX_DIM),
      f_bias.reshape(1, N_HEADS), dt_bias.reshape(1, N_HEADS))


N_EXTRA = 3
Q_EXTRA_SPLIT = 2 * N_EXTRA
PAIR_EXTRA = 16


def _placement_constants():
    pq = np.zeros((N_EXTRA, N_HEADS, LANES), np.float32)
    pk = np.zeros((N_EXTRA, N_HEADS, LANES), np.float32)
    oq = np.zeros((1, LANES), np.float32)
    ok = np.zeros((1, LANES), np.float32)
    for h in range(N_HEADS):
        base = PAIR_EXTRA * (h // 2)
        for c in range(N_EXTRA):
            if h % 2 == 0:
                pq[c, h, base + c] = 1.0
                pk[c, h, base + N_EXTRA + c] = -1.0
                oq[0, base + N_EXTRA + c] = 1.0
                ok[0, base + c] = 1.0
            else:
                pq[c, h, base + 3 * N_EXTRA + c] = 1.0
                pk[c, h, base + 2 * N_EXTRA + c] = -1.0
                oq[0, base + 2 * N_EXTRA + c] = 1.0
                ok[0, base + 3 * N_EXTRA + c] = 1.0
    return (jnp.asarray(pq, BF16), jnp.asarray(pk, BF16), jnp.asarray(oq), jnp.asarray(ok))


def _fbias_kernel(logf_ref, pq_ref, pk_ref, oq_ref, ok_ref, kx_ref, qx_ref, carry_ref, *, blk, nblk):
    r = lax.broadcasted_iota(jnp.int32, (blk, blk), 0)
    c = lax.broadcasted_iota(jnp.int32, (blk, blk), 1)
    ltri = jnp.where(r >= c, 1.0, 0.0).astype(BF16)

    @pl.when(pl.program_id(1) == 0)
    def _():
        carry_ref[...] = jnp.zeros(carry_ref.shape, F32)

    lane = lax.broadcasted_iota(jnp.int32, (blk, N_EXTRA * N_HEADS), 1)

    def body(i, carry):
        off = pl.multiple_of(i * blk, blk)
        f = _sel_dot(ltri, logf_ref[0, pl.ds(off, blk), :]) + carry
        p1, p2, p3 = _split3(f * LOG2E)
        pieces = jnp.where(lane < N_HEADS, p1, jnp.where(lane < 2 * N_HEADS, p2, p3))
        kx_ref[0, pl.ds(off, blk), :] = (ok_ref[...] + _dot(pieces, pk_ref[...])).astype(BF16)
        qx_ref[0, pl.ds(off, blk), :] = (oq_ref[...] + _dot(pieces, pq_ref[...])).astype(BF16)
        return f[blk - 1:blk, :]

    carry_ref[...] = lax.fori_loop(0, nblk, body, carry_ref[...], unroll=True)


def _fbias(logf3, blk):
    b, t, _ = logf3.shape
    tt = t if t <= 4096 else 2048
    assert t % tt == 0 and tt % blk == 0
    pq, pk, oq, ok = _placement_constants()
    width = N_EXTRA * N_HEADS
    out = jax.ShapeDtypeStruct((b, t, LANES), BF16)
    tile = lambda w: pl.BlockSpec((1, tt, w), lambda bi, ti: (bi, ti, 0))
    return pl.pallas_call(
        functools.partial(_fbias_kernel, blk=blk, nblk=tt // blk),
        grid=(b, t // tt),
        in_specs=[tile(width), _resident((width, LANES)), _resident((width, LANES)),
                  _resident((1, LANES)), _resident((1, LANES))],
        out_specs=(tile(LANES), tile(LANES)),
        out_shape=(out, out),
        scratch_shapes=[pltpu.VMEM((1, width), F32)],
        compiler_params=_params(("arbitrary", "arbitrary")),
        name="fbias",
    )(jnp.tile(logf3, (1, 1, N_EXTRA)), pq.reshape(width, LANES), pk.reshape(width, LANES), oq, ok)


def _pair_selectors():
    sel = np.zeros((N_HEADS // 2, LANES, LANES), np.float32)
    for p in range(N_HEADS // 2):
        for e in range(PAIR_EXTRA):
            sel[p, PAIR_EXTRA * p + e, e] = 1.0
    return jnp.asarray(sel, BF16)


def _attn_kernel(q_ref, qx_ref, k_ref, kx_ref, sel_ref, vt_ref, o_ref, kcat_ref, qs_ref, acc_ref, sa_ref, sb_ref,
                 *, tq, tsub, tk, past, tiles):
    i = pl.program_id(2)
    nsub = tq // tsub
    chains = [(sb, a) for sb in range(nsub) for a in range(2)]
    sel = sel_ref[0]

    @pl.when(i == 0)
    def _():
        kcat_ref[:, 0:LANES] = k_ref[0]

        def fill(r, carry):
            off = pl.multiple_of(r * tk, tk)
            kcat_ref[pl.ds(off, tk), LANES:2 * LANES] = _dot(kx_ref[0, pl.ds(off, tk), :], sel).astype(BF16)
            return carry

        lax.fori_loop(0, kcat_ref.shape[0] // tk, fill, 0)

    lane = lax.broadcasted_iota(jnp.int32, (tsub, LANES), 1)
    nct = len(chains)
    for h in range(tiles):
        for sb in range(nsub):
            rows = slice(tq * h + tsub * sb, tq * h + tsub * (sb + 1))
            q2 = q_ref[0, rows, :].astype(F32)
            qx2 = _dot(qx_ref[0, rows, :], sel)
            zero = jnp.zeros_like(q2)
            for a, keep_q, keep_x in ((0, lane < HEAD_DIM, lane < Q_EXTRA_SPLIT),
                                      (1, lane >= HEAD_DIM, lane >= Q_EXTRA_SPLIT)):
                c = nct * h + 2 * sb + a
                qs_ref[c, 0:LANES, :] = jnp.where(keep_q, q2, zero).T.astype(BF16)
                qs_ref[c, LANES:2 * LANES, :] = jnp.where(keep_x, qx2, zero).T.astype(BF16)
    acc_ref[...] = jnp.zeros(acc_ref.shape, F32)

    def qk_stage(j, s_ref, h, masked=False):
        off = pl.multiple_of(j * tk, tk)
        kc = kcat_ref[pl.ds(off, tk), :]
        q_lo = past + (i * tiles + h) * tq
        mx = []
        for c, (sb, a) in enumerate(chains):
            s = _dot(kc, qs_ref[nct * h + c])
            if masked:
                kpos = off + lax.broadcasted_iota(jnp.int32, (tk, tsub), 0)
                qpos = q_lo + tsub * sb + lax.broadcasted_iota(jnp.int32, (tk, tsub), 1)
                s = jnp.where(kpos <= qpos, s, NEG_BIG)
            s_ref[c] = s
            mx.append(jnp.max(s, axis=0, keepdims=True))
        return tuple(mx)

    def sm_stage(j, s_ref, mx, state, h):
        off = pl.multiple_of(j * tk, tk)
        vts = [vt_ref[0, HEAD_DIM * a:HEAD_DIM * (a + 1), pl.ds(off, tk)] for a in range(2)]
        out, probs = [], []
        for c, (sb, a) in enumerate(chains):
            m_prev, l_prev = state[c]
            m_new = jnp.maximum(m_prev, mx[c])
            alpha = jnp.exp2(m_prev - m_new)
            p = jnp.exp2(s_ref[c] - m_new)
            out.append((m_new, alpha * l_prev + jnp.sum(p, axis=0, keepdims=True)))
            probs.append((alpha, p.astype(BF16)))
        for c, (sb, a) in enumerate(chains):
            alpha, p = probs[c]
            acc_ref[nct * h + c] = alpha * acc_ref[nct * h + c] + _dot(vts[a], p)
        return tuple(out)

    def finalize(state, h):
        for sb in range(nsub):
            ot = jnp.concatenate([acc_ref[nct * h + 2 * sb + a] * (1.0 / state[2 * sb + a][1]) for a in range(2)],
                                 axis=0)
            rows = slice(tq * h + tsub * sb, tq * h + tsub * (sb + 1))
            o_ref[0, rows, :] = ot.T.astype(BF16)

    def pair_loop(n_pairs, first, h, carry):
        def body(u, carry):
            jprev, mx_a, state = carry
            mx_b = qk_stage(first + 2 * u, sb_ref, h)
            state = sm_stage(jprev, sa_ref, mx_a, state, h)
            mx_a = qk_stage(first + 2 * u + 1, sa_ref, h)
            state = sm_stage(first + 2 * u, sb_ref, mx_b, state, h)
            return first + 2 * u + 1, mx_a, state
        return lax.fori_loop(0, n_pairs, body, carry)

    init = tuple((jnp.full((1, tsub), NEG_BIG, F32), jnp.zeros((1, tsub), F32)) for _ in chains)
    if tiles == 1:
        jm = (past + i * tq) // tk
        mx0 = qk_stage(jm, sa_ref, 0, masked=True)
        carry = pair_loop(jm // 2, 0, 0, (jm, mx0, init))

        def odd_tail(args):
            jprev, mx_a, state = args
            mx_b = qk_stage(jm - 1, sb_ref, 0)
            state = sm_stage(jprev, sa_ref, mx_a, state, 0)
            return sm_stage(jm - 1, sb_ref, mx_b, state, 0)

        def even_tail(args):
            jprev, mx_a, state = args
            return sm_stage(jprev, sa_ref, mx_a, state, 0)

        finalize(lax.cond(jm % 2 == 1, odd_tail, even_tail, carry), 0)
    else:
        ja, jb = 2 * i, 2 * i + 1
        mx0 = qk_stage(ja, sa_ref, 0, masked=True)
        jprev, mx_a, st_a = pair_loop(i, 0, 0, (ja, mx0, init))
        mx_b = qk_stage(jb, sb_ref, 1, masked=True)
        st_a = sm_stage(jprev, sa_ref, mx_a, st_a, 0)
        mx_a = qk_stage(0, sa_ref, 1)
        st_b = sm_stage(jb, sb_ref, mx_b, init, 1)
        finalize(st_a, 0)
        jprev, mx_a, st_b = pair_loop(i, 1, 1, (0, mx_a, st_b))
        finalize(sm_stage(jprev, sa_ref, mx_a, st_b, 1), 1)


def _attention(q, qx, k, kx, vt, *, past, tq, tk):
    b, t, _ = q.shape
    tkeys = k.shape[1]
    npairs = N_HEADS // 2
    assert t % tq == 0 and tkeys % tk == 0 and past % tq == 0 and tk % tq == 0
    tsub = min(tq, MXU_DIM)
    assert tq % tsub == 0
    tiles = 2 if (past == 0 and tq == tk and t % (2 * tq) == 0) else 1
    rows = tiles * tq
    nchain = 2 * (tq // tsub)
    qoff = past // rows
    assert past % rows == 0
    nbuf = pl.Buffered(1 if t // rows > 1 else 2)
    return pl.pallas_call(
        functools.partial(_attn_kernel, tq=tq, tsub=tsub, tk=tk, past=past, tiles=tiles),
        grid=(b, npairs, t // rows),
        in_specs=[pl.BlockSpec((1, rows, LANES), lambda bi, p, i: (bi, i, p)),
                  pl.BlockSpec((1, rows, LANES), lambda bi, p, i: (bi, qoff + i, 0)),
                  pl.BlockSpec((1, tkeys, LANES), lambda bi, p, i: (bi, 0, p), pipeline_mode=nbuf),
                  pl.BlockSpec((1, tkeys, LANES), lambda bi, p, i: (bi, 0, 0), pipeline_mode=nbuf),
                  pl.BlockSpec((1, LANES, LANES), lambda bi, p, i: (p, 0, 0)),
                  pl.BlockSpec((1, LANES, tkeys), lambda bi, p, i: (bi, p, 0), pipeline_mode=nbuf)],
        out_specs=pl.BlockSpec((1, rows, LANES), lambda bi, p, i: (bi, i, p)),
        out_shape=jax.ShapeDtypeStruct((b, t, FOX_DIM), BF16),
        scratch_shapes=[pltpu.VMEM((tkeys, 2 * LANES), BF16),
                        pltpu.VMEM((tiles * nchain, 2 * LANES, tsub), BF16),
                        pltpu.VMEM((tiles * nchain, HEAD_DIM, tsub), F32),
                        pltpu.VMEM((nchain, tk, tsub), F32),
                        pltpu.VMEM((nchain, tk, tsub), F32)],
        compiler_params=_params(("arbitrary", "arbitrary", "arbitrary")),
        name="fox_attention",
    )(q, qx, k, kx, _pair_selectors(), vt)


def _ssd_kernel(xbc_ref, z_ref, dt_ref, cprev_ref, sprev_ref, wconv_ref, bconv_ref, alog_ref,
                dskip_ref, gssd_ref, rexp_ref, rexpt_ref,
                y_ref, hfin_ref, ext_ref, state_ref, *, L):
    c = pl.program_id(1)

    @pl.when(c == 0)
    def _():
        state_ref[...] = sprev_ref[0]
        ext_ref[...] = cprev_ref[0]

    x_tile = xbc_ref[0]
    ext = jnp.concatenate([ext_ref[...], x_tile], axis=0)
    w = wconv_ref[...]
    pre = bconv_ref[...] + x_tile * w[CONV_W - 1:CONV_W, :]
    for back in range(1, CONV_W):
        shifted = pltpu.roll(ext, back, axis=0)[SUBLANES:, :]
        pre = pre + shifted * w[CONV_W - 1 - back:CONV_W - back, :]
    ext_ref[...] = x_tile[L - SUBLANES:L, :]
    u = pre * _sigmoid(pre)
    xs = u[:, 0:SSD_DIM]
    bmat = u[:, SSD_DIM:SSD_DIM + SSD_GROUPS * SSD_STATE].astype(BF16)
    cmat = u[:, SSD_DIM + SSD_GROUPS * SSD_STATE:].astype(BF16)

    dt = dt_ref[0]
    a = dt * (-jnp.exp(alog_ref[...]))
    r = lax.broadcasted_iota(jnp.int32, (L, L), 0)
    cc = lax.broadcasted_iota(jnp.int32, (L, L), 1)
    causal = r >= cc
    ltri = jnp.where(causal, 1.0, 0.0).astype(BF16)
    cum = _sel_dot(ltri, a)
    e_r = lax.broadcasted_iota(jnp.int32, (N_HEADS, N_HEADS), 0)
    e_c = lax.broadcasted_iota(jnp.int32, (N_HEADS, N_HEADS), 1)
    eye = jnp.where(e_r == e_c, 1.0, 0.0).astype(BF16)
    cum_t = _sel_dot_nt(eye, cum)
    last = cum[L - 1:L, :]

    rexp = rexp_ref[...]
    dtx = _dot_sel(dt, rexp)
    wendx = _dot_sel(jnp.exp(last - cum), rexp)
    ecumx = _dot_sel(jnp.exp(cum), rexp)
    xdt = xs * dtx
    xdt_b = xdt.astype(BF16)
    xw_b = (xdt * wendx).astype(BF16)
    elast = jnp.exp(cum_t[:, L - 1:L])
    elast_b = jnp.broadcast_to(elast, (N_HEADS, SSD_STATE))
    rexpt = rexpt_ref[...]
    lane = lax.broadcasted_iota(jnp.int32, (L, LANES), 1)

    heads_per_group = N_HEADS // SSD_GROUPS
    ys = []
    for g in range(SSD_GROUPS):
        gs = slice(GROUP_DIM * g, GROUP_DIM * (g + 1))
        cg = cmat[:, SSD_STATE * g:SSD_STATE * (g + 1)]
        bg = bmat[:, SSD_STATE * g:SSD_STATE * (g + 1)]
        cb = _dot_nt(cg, bg)
        st = state_ref[gs, :]
        yoff = _dot_nt(cg, st.astype(BF16))
        for pr in range(heads_per_group // 2):
            ps = slice(GROUP_DIM * g + LANES * pr, GROUP_DIM * g + LANES * (pr + 1))
            slab = xdt_b[:, ps]
            res = []
            for hh in range(2):
                hd = heads_per_group * g + 2 * pr + hh
                seg = cum[:, hd:hd + 1] - cum_t[hd:hd + 1, :]
                dec = jnp.exp(jnp.where(causal, seg, -jnp.inf))
                res.append(_dot((cb * dec).astype(BF16), slab))
            ydiag = jnp.where(lane < HEAD_DIM, res[0], res[1])
            ys.append(ydiag + yoff[:, LANES * pr:LANES * (pr + 1)] * ecumx[:, ps])
        snew = _dot_tn(xw_b[:, gs], bg)
        scale = _sel_dot(rexpt[gs, :], elast_b)
        state_ref[gs, :] = scale * st + snew

    y = jnp.concatenate(ys, axis=-1) + dskip_ref[...] * xs
    zt = z_ref[0]
    ug = y * (zt * _sigmoid(zt))
    outs = []
    for g in range(SSD_GROUPS):
        ugg = ug[:, GROUP_DIM * g:GROUP_DIM * (g + 1)]
        outs.append(ugg * lax.rsqrt(jnp.mean(ugg * ugg, axis=-1, keepdims=True) + EPS))
    y_ref[0] = (jnp.concatenate(outs, axis=-1) * gssd_ref[...]).astype(BF16)

    @pl.when(c == pl.num_programs(1) - 1)
    def _():
        hfin_ref[0] = state_ref[...]


def _ssd(xbc, z, dt, conv_prev8, ssm_prev, w_conv, b_conv, a_log, d_skip, g_ssd, *, L):
    b, t, _ = xbc.shape
    assert t % L == 0 and L % SUBLANES == 0
    head_of_lane = jnp.arange(SSD_DIM) // HEAD_DIM
    rexp = (jnp.arange(N_HEADS)[:, None] == head_of_lane[None, :]).astype(BF16)
    tile = lambda w: pl.BlockSpec((1, L, w), lambda bi, ci: (bi, ci, 0))
    per_b = lambda s: pl.BlockSpec((1,) + s, lambda bi, ci: (bi, 0, 0))
    return pl.pallas_call(
        functools.partial(_ssd_kernel, L=L),
        grid=(b, t // L),
        in_specs=[tile(CONV_DIM), tile(SSD_DIM), tile(N_HEADS),
                  per_b((SUBLANES, CONV_DIM)), per_b((SSD_DIM, SSD_STATE)),
                  _resident((CONV_W, CONV_DIM)), _resident((1, CONV_DIM)), _resident((1, N_HEADS)),
                  _resident((1, SSD_DIM)), _resident((1, SSD_DIM)),
                  _resident((N_HEADS, SSD_DIM)), _resident((SSD_DIM, N_HEADS))],
        out_specs=(tile(SSD_DIM), per_b((SSD_DIM, SSD_STATE))),
        out_shape=(jax.ShapeDtypeStruct((b, t, SSD_DIM), BF16),
                   jax.ShapeDtypeStruct((b, SSD_DIM, SSD_STATE), F32)),
        scratch_shapes=[pltpu.VMEM((SUBLANES, CONV_DIM), F32),
                        pltpu.VMEM((SSD_DIM, SSD_STATE), F32)],
        compiler_params=_params(("arbitrary", "arbitrary")),
        name="ssd_scan",
    )(xbc, z, dt, conv_prev8, ssm_prev, w_conv, b_conv.reshape(1, CONV_DIM), a_log.reshape(1, N_HEADS),
      jnp.repeat(d_skip, HEAD_DIM).reshape(1, SSD_DIM), g_ssd.reshape(1, SSD_DIM), rexp, rexp.T)


def _ffn_kernel(x_ref, attn_ref, y_ref, wo_ref, g_ref, wg_ref, wu_ref, wd_ref, o_ref, hf_ref):
    j = pl.program_id(1)

    @pl.when(j == 0)
    def _():
        mix = _dot(attn_ref[...], wo_ref[0:FOX_DIM, :]) + _dot(y_ref[...], wo_ref[FOX_DIM:, :])
        x = x_ref[...] + mix
        ms = jnp.mean(x * x, axis=-1, keepdims=True)
        hf_ref[...] = (x * lax.rsqrt(ms + EPS) * g_ref[...]).astype(BF16)
        o_ref[...] = x

    hf = hf_ref[...]
    gate = _dot(hf, wg_ref[...])
    act = (gate * _sigmoid(gate) * _dot(hf, wu_ref[...])).astype(BF16)
    o_ref[...] += _dot(act, wd_ref[...])


def _outproj_ffn(x2d, attn, y, w_out, g_ffn, w_gate, w_up, w_down):
    n, d = x2d.shape
    dff = w_gate.shape[1]
    tm, tf = min(512, n), 512
    assert n % tm == 0 and dff % tf == 0
    row = lambda w: pl.BlockSpec((tm, w), lambda i, j: (i, 0))
    return pl.pallas_call(
        _ffn_kernel,
        grid=(n // tm, dff // tf),
        in_specs=[row(d), row(FOX_DIM), row(SSD_DIM), _resident((FOX_DIM + SSD_DIM, d)),
                  _resident((1, d)),
                  pl.BlockSpec((d, tf), lambda i, j: (0, j)),
                  pl.BlockSpec((d, tf), lambda i, j: (0, j)),
                  pl.BlockSpec((tf, d), lambda i, j: (j, 0))],
        out_specs=pl.BlockSpec((tm, d), lambda i, j: (i, 0)),
        out_shape=jax.ShapeDtypeStruct((n, d), F32),
        scratch_shapes=[pltpu.VMEM((tm, d), BF16)],
        compiler_params=_params(("arbitrary", "arbitrary")),
        name="ffn",
    )(x2d, attn, y, w_out, g_ffn.reshape(1, d), w_gate, w_up, w_down)


def _pack_w_in(w_in):
    d = w_in.shape[0]
    o_f = 3 * FOX_DIM
    o_z = o_f + N_HEADS
    o_xbc = o_z + SSD_DIM
    o_dt = o_xbc + CONV_DIM
    pad = jnp.zeros((d, LANES - N_HEADS), w_in.dtype)
    return jnp.concatenate([w_in[:, :o_f], w_in[:, o_z:o_xbc], w_in[:, o_xbc:o_dt],
                            w_in[:, o_f:o_z], pad, w_in[:, o_dt:], pad], axis=1).astype(BF16)


def _layer(x, conv_prev, ssm_prev, k_prev, v_prev, logf_prev, wts, *, attn_tq, attn_tk, fbias_blk, ssd_chunk):
    (g_mix, w_all, f_bias, g_q, g_k, w_conv, b_conv, dt_bias, a_log, d_skip, g_ssd,
     w_out, g_ffn, w_gate, w_up, w_down) = wts
    b, t, d = x.shape
    n = b * t
    x2d = x.reshape(n, d)
    qb, k, kb, v, vb, z, xbc, logf, dt = _inproj(x2d, t, g_mix, w_all, g_q, g_k, f_bias, dt_bias)

    past = 0 if k_prev is None else k_prev.shape[1]
    kb3, logf3 = kb.reshape(b, t, FOX_DIM), logf.reshape(b, t, N_HEADS)
    vt3 = vb if vb.ndim == 3 else jnp.transpose(vb.reshape(b, t, FOX_DIM), (0, 2, 1))
    logf_all = logf3
    if past:
        kb3 = jnp.concatenate([k_prev.reshape(b, past, FOX_DIM).astype(BF16), kb3], axis=1)
        vt_prev = jnp.transpose(v_prev.reshape(b, past, FOX_DIM).astype(BF16), (0, 2, 1))
        vt3 = jnp.concatenate([vt_prev, vt3], axis=2)
        logf_all = jnp.concatenate([logf_prev.astype(F32), logf3], axis=1)
    tkeys = past + t
    tk = attn_tk
    if tkeys % tk:
        tk = -(-tkeys // LANES) * LANES
        grow = ((0, 0), (0, tk - tkeys), (0, 0))
        kb3, logf_all = jnp.pad(kb3, grow), jnp.pad(logf_all, grow)
        vt3 = jnp.pad(vt3, ((0, 0), (0, 0), (0, tk - tkeys)))
    kx, qx = _fbias(logf_all, min(fbias_blk, tk))
    attn = _attention(qb.reshape(b, t, FOX_DIM), qx, kb3, kx, vt3, past=past, tq=attn_tq, tk=tk)

    conv_prev8 = jnp.pad(conv_prev.astype(F32), ((0, 0), (SUBLANES - (CONV_W - 1), 0), (0, 0)))
    y, h_last = _ssd(xbc.reshape(b, t, CONV_DIM), z.reshape(b, t, SSD_DIM), dt.reshape(b, t, N_HEADS),
                     conv_prev8, ssm_prev.astype(F32).reshape(b, SSD_DIM, SSD_STATE),
                     w_conv, b_conv, a_log, d_skip, g_ssd, L=ssd_chunk)

    out = _outproj_ffn(x2d, attn.reshape(n, FOX_DIM), y.reshape(n, SSD_DIM), w_out, g_ffn, w_gate, w_up, w_down)

    xbc_all = jnp.concatenate([conv_prev.astype(F32), xbc.reshape(b, t, CONV_DIM)], axis=1)
    return (out.reshape(b, t, d), xbc_all[:, -(CONV_W - 1):],
            h_last.reshape(b, N_HEADS, HEAD_DIM, SSD_STATE),
            k.reshape(b, t, N_HEADS, HEAD_DIM), v.reshape(b, t, N_HEADS, HEAD_DIM), logf3)


def kernel(x_prompt, x_sample, cache_conv, state_ssm, cache_fox_k, cache_fox_v, cache_fox_logf, g_mix, w_in, f_bias, g_q, g_k, w_conv, b_conv, dt_bias, a_log, d_skip, g_ssd, w_out, g_ffn, w_gate, w_up, w_down):
    depth = g_mix.shape[0]
    yp, ys = x_prompt, x_sample
    bp = x_prompt.shape[0]
    outs_p = [[] for _ in range(5)]
    outs_s = [[] for _ in range(5)]
    for i in range(depth):
        wts = (g_mix[i], _pack_w_in(w_in[i]), f_bias[i], g_q[i], g_k[i], w_conv[i], b_conv[i], dt_bias[i],
               a_log[i], d_skip[i], g_ssd[i], w_out[i].astype(BF16), g_ffn[i],
               w_gate[i].astype(BF16), w_up[i].astype(BF16), w_down[i].astype(BF16))
        tp = yp.shape[1]
        yp, *rest = _layer(
            yp, jnp.zeros((bp, CONV_W - 1, CONV_DIM), F32),
            jnp.zeros((bp, N_HEADS, HEAD_DIM, SSD_STATE), F32), None, None, None, wts,
            attn_tq=min(1024, tp), attn_tk=min(1024, tp), fbias_blk=256, ssd_chunk=min(128, tp))
        for lst, val in zip(outs_p, rest):
            lst.append(val)
        tsamp = ys.shape[1]
        ys, *rest = _layer(
            ys, cache_conv[i], state_ssm[i], cache_fox_k[i], cache_fox_v[i], cache_fox_logf[i], wts,
            attn_tq=tsamp, attn_tk=LANES, fbias_blk=LANES, ssd_chunk=tsamp)
        for lst, val in zip(outs_s, rest):
            lst.append(val)
    return (yp, ys, *[jnp.stack(l) for l in outs_p], *[jnp.stack(l) for l in outs_s])
```

```python
import functools

import jax
import jax.numpy as jnp
import numpy as np
from jax import lax
from jax.experimental import pallas as pl
from jax.experimental.pallas import tpu as pltpu

F32 = jnp.float32
BF16 = jnp.bfloat16

EPS = 1e-6
HEAD_DIM = 64
N_HEADS = 16
FOX_DIM = 1024
SSD_DIM = 1024
SSD_GROUPS = 2
SSD_STATE = 128
GROUP_DIM = SSD_DIM // SSD_GROUPS
CONV_W = 4
CONV_DIM = SSD_DIM + 2 * SSD_GROUPS * SSD_STATE
LANES = 128
SUBLANES = 8
MXU_DIM = 256
NEG_BIG = -1e30
LOG2E = 1.4426950408889634
VMEM_LIMIT = 56 * 1024 * 1024
INPROJ_TOKENS = MXU_DIM
FFN_TOKENS = 2 * MXU_DIM
FFN_FF_TILE = 2 * MXU_DIM
FBIAS_TIME_TILE = 8 * MXU_DIM

COL_Q, COL_K, COL_V, COL_Z = 0, FOX_DIM, 2 * FOX_DIM, 3 * FOX_DIM
COL_XBC = COL_Z + SSD_DIM
COL_F = COL_XBC + CONV_DIM
COL_DT = COL_F + LANES
IN_COLS = COL_DT + LANES


def _dot(a, b):
    return jnp.dot(a, b, preferred_element_type=F32)


def _dot_nt(a, b):
    return lax.dot_general(a, b, (((1,), (1,)), ((), ())), preferred_element_type=F32)


def _dot_tn(a, b):
    return lax.dot_general(a, b, (((0,), (0,)), ((), ())), preferred_element_type=F32)


def _split3(x):
    x1 = x.astype(BF16)
    r = x - x1.astype(F32)
    x2 = r.astype(BF16)
    r = r - x2.astype(F32)
    return x1, x2, r.astype(BF16)


def _sel_dot(sel, x):
    x1, x2, x3 = _split3(x)
    return _dot(sel, x1) + _dot(sel, x2) + _dot(sel, x3)


def _dot_sel2(x, sel):
    x1 = x.astype(BF16)
    x2 = (x - x1.astype(F32)).astype(BF16)
    return _dot(x1, sel) + _dot(x2, sel)


def _sel_dot_nt(sel, x):
    x1, x2, x3 = _split3(x)
    return _dot_nt(sel, x1) + _dot_nt(sel, x2) + _dot_nt(sel, x3)


def _sigmoid(x):
    return 0.5 * jnp.tanh(0.5 * x) + 0.5


def _softplus(x):
    return jnp.maximum(x, 0.0) + jnp.log1p(jnp.exp(-jnp.abs(x)))


def _resident(shape):
    nd = len(shape)
    return pl.BlockSpec(shape, lambda *_: (0,) * nd, pipeline_mode=pl.Buffered(1))


def _params(sem):
    return pltpu.CompilerParams(dimension_semantics=sem, vmem_limit_bytes=VMEM_LIMIT)


def _inproj_kernel(x_ref, gmix_ref, w_ref, e_ref, gq_ref, gk_ref, fb_ref, dtb_ref,
                   q_ref, k_ref, kb_ref, v_ref, vb_ref, z_ref, xbc_ref, logf_ref, dt_ref, *, v_transposed):
    x = x_ref[...]
    ms = jnp.mean(x * x, axis=-1, keepdims=True)
    h = (x * lax.rsqrt(ms + EPS) * gmix_ref[...]).astype(BF16)
    e = e_ref[...]

    def head_rms(y, g):
        ysq = y * y
        hi = ysq.astype(BF16)
        lo = (ysq - hi.astype(F32)).astype(BF16)
        parts = []
        for c in range(FOX_DIM // MXU_DIM):
            sl = slice(MXU_DIM * c, MXU_DIM * (c + 1))
            parts.append(_dot(hi[:, sl], e) + _dot(lo[:, sl], e))
        ss = jnp.concatenate(parts, axis=-1)
        return y * lax.rsqrt(ss * (1.0 / HEAD_DIM) + EPS) * g

    q = _dot(h, w_ref[:, COL_Q:COL_Q + FOX_DIM])
    q_ref[...] = (head_rms(q, gq_ref[...]) * (HEAD_DIM ** -0.5 * LOG2E)).astype(BF16)
    k = head_rms(_dot(h, w_ref[:, COL_K:COL_K + FOX_DIM]), gk_ref[...])
    for hd in range(N_HEADS):
        k_ref[:, hd, :] = k[:, HEAD_DIM * hd:HEAD_DIM * (hd + 1)]
    kb_ref[...] = k.astype(BF16)
    v = _dot(h, w_ref[:, COL_V:COL_V + FOX_DIM])
    for hd in range(N_HEADS):
        v_ref[:, hd, :] = v[:, HEAD_DIM * hd:HEAD_DIM * (hd + 1)]
    if v_transposed:
        vb_ref[0] = v.T.astype(BF16)
    else:
        vb_ref[...] = v.astype(BF16)
    z_ref[...] = _dot(h, w_ref[:, COL_Z:COL_Z + SSD_DIM])
    xbc_ref[...] = _dot(h, w_ref[:, COL_XBC:COL_XBC + CONV_DIM])
    f_raw = _dot(h, w_ref[:, COL_F:COL_F + LANES])[:, :N_HEADS]
    logf_ref[...] = -_softplus(-(f_raw + fb_ref[...]))
    dt_raw = _dot(h, w_ref[:, COL_DT:COL_DT + LANES])[:, :N_HEADS]
    dt_ref[...] = _softplus(dt_raw + dtb_ref[...])


def _inproj(x2d, seq, g_mix, w_all, g_q, g_k, f_bias, dt_bias):
    n, d = x2d.shape
    tm = min(INPROJ_TOKENS, n)
    assert n % tm == 0 and n % seq == 0
    v_transposed = seq % tm == 0
    if v_transposed:
        tiles = seq // tm
        vb_shape = jax.ShapeDtypeStruct((n // seq, FOX_DIM, seq), BF16)
        vb_spec = pl.BlockSpec((1, FOX_DIM, tm), lambda i: (i // tiles, 0, i % tiles))
    else:
        vb_shape = jax.ShapeDtypeStruct((n, FOX_DIM), BF16)
        vb_spec = pl.BlockSpec((tm, FOX_DIM), lambda i: (i, 0))
    blk = jnp.arange(MXU_DIM) // HEAD_DIM
    e = (blk[:, None] == blk[None, :]).astype(BF16)
    row = lambda w: pl.BlockSpec((tm, w), lambda i: (i, 0))
    heads = pl.BlockSpec((tm, N_HEADS, HEAD_DIM), lambda i: (i, 0, 0))
    out_shapes = (
        jax.ShapeDtypeStruct((n, FOX_DIM), BF16),
        jax.ShapeDtypeStruct((n, N_HEADS, HEAD_DIM), F32),
        jax.ShapeDtypeStruct((n, FOX_DIM), BF16),
        jax.ShapeDtypeStruct((n, N_HEADS, HEAD_DIM), F32),
        vb_shape,
        jax.ShapeDtypeStruct((n, SSD_DIM), F32),
        jax.ShapeDtypeStruct((n, CONV_DIM), F32),
        jax.ShapeDtypeStruct((n, N_HEADS), F32),
        jax.ShapeDtypeStruct((n, N_HEADS), F32),
    )
    return pl.pallas_call(
        functools.partial(_inproj_kernel, v_transposed=v_transposed),
        grid=(n // tm,),
        in_specs=[row(d), _resident((1, d)), _resident((d, IN_COLS)), _resident((MXU_DIM, MXU_DIM)),
                  _resident((1, FOX_DIM)), _resident((1, FOX_DIM)),
                  _resident((1, N_HEADS)), _resident((1, N_HEADS))],
        out_specs=(row(FOX_DIM), heads, row(FOX_DIM), heads, vb_spec,
                   row(SSD_DIM), row(CONV_DIM), row(N_HEADS), row(N_HEADS)),
        out_shape=out_shapes,
        compiler_params=_params(("arbitrary",)),
        name="inproj",
    )(x2d, g_mix.reshape(1, d), w_all, e,
      jnp.tile(g_q, N_HEADS).reshape(1, FOX_DIM), jnp.tile(g_k, N_HEADS).reshape(1, FOX_DIM),
      f_bias.reshape(1, N_HEADS), dt_bias.reshape(1, N_HEADS))


N_EXTRA = 3
Q_EXTRA_SPLIT = 2 * N_EXTRA
PAIR_EXTRA = 16


def _placement_constants():
    pq = np.zeros((N_EXTRA, N_HEADS, LANES), np.float32)
    pk = np.zeros((N_EXTRA, N_HEADS, LANES), np.float32)
    oq = np.zeros((1, LANES), np.float32)
    ok = np.zeros((1, LANES), np.float32)
    for h in range(N_HEADS):
        base = PAIR_EXTRA * (h // 2)
        for c in range(N_EXTRA):
            if h % 2 == 0:
                pq[c, h, base + c] = 1.0
                pk[c, h, base + N_EXTRA + c] = -1.0
                oq[0, base + N_EXTRA + c] = 1.0
                ok[0, base + c] = 1.0
            else:
                pq[c, h, base + 3 * N_EXTRA + c] = 1.0
                pk[c, h, base + 2 * N_EXTRA + c] = -1.0
                oq[0, base + 2 * N_EXTRA + c] = 1.0
                ok[0, base + 3 * N_EXTRA + c] = 1.0
    return (jnp.asarray(pq, BF16), jnp.asarray(pk, BF16), jnp.asarray(oq), jnp.asarray(ok))


def _fbias_kernel(logf_ref, pq_ref, pk_ref, oq_ref, ok_ref, kx_ref, qx_ref, carry_ref, *, blk, nblk):
    r = lax.broadcasted_iota(jnp.int32, (blk, blk), 0)
    c = lax.broadcasted_iota(jnp.int32, (blk, blk), 1)
    ltri = jnp.where(r >= c, 1.0, 0.0).astype(BF16)

    @pl.when(pl.program_id(1) == 0)
    def _():
        carry_ref[...] = jnp.zeros(carry_ref.shape, F32)

    lane = lax.broadcasted_iota(jnp.int32, (blk, N_EXTRA * N_HEADS), 1)

    def body(i, carry):
        off = pl.multiple_of(i * blk, blk)
        f = _sel_dot(ltri, logf_ref[0, pl.ds(off, blk), :]) + carry
        p1, p2, p3 = _split3(f * LOG2E)
        pieces = jnp.where(lane < N_HEADS, p1, jnp.where(lane < 2 * N_HEADS, p2, p3))
        kx_ref[0, pl.ds(off, blk), :] = (ok_ref[...] + _dot(pieces, pk_ref[...])).astype(BF16)
        qx_ref[0, pl.ds(off, blk), :] = (oq_ref[...] + _dot(pieces, pq_ref[...])).astype(BF16)
        return f[blk - 1:blk, :]

    carry_ref[...] = lax.fori_loop(0, nblk, body, carry_ref[...], unroll=True)


def _fbias(logf3, blk):
    b, t, _ = logf3.shape
    tt = t if t <= 2 * FBIAS_TIME_TILE else FBIAS_TIME_TILE
    assert t % tt == 0 and tt % blk == 0
    pq, pk, oq, ok = _placement_constants()
    width = N_EXTRA * N_HEADS
    out = jax.ShapeDtypeStruct((b, t, LANES), BF16)
    tile = lambda w: pl.BlockSpec((1, tt, w), lambda bi, ti: (bi, ti, 0))
    return pl.pallas_call(
        functools.partial(_fbias_kernel, blk=blk, nblk=tt // blk),
        grid=(b, t // tt),
        in_specs=[tile(width), _resident((width, LANES)), _resident((width, LANES)),
                  _resident((1, LANES)), _resident((1, LANES))],
        out_specs=(tile(LANES), tile(LANES)),
        out_shape=(out, out),
        scratch_shapes=[pltpu.VMEM((1, width), F32)],
        compiler_params=_params(("arbitrary", "arbitrary")),
        name="fbias",
    )(jnp.tile(logf3, (1, 1, N_EXTRA)), pq.reshape(width, LANES), pk.reshape(width, LANES), oq, ok)


def _pair_selectors():
    sel = np.zeros((N_HEADS // 2, LANES, LANES), np.float32)
    for p in range(N_HEADS // 2):
        for e in range(PAIR_EXTRA):
            sel[p, PAIR_EXTRA * p + e, e] = 1.0
    return jnp.asarray(sel, BF16)


def _attn_kernel(q_ref, qx_ref, k_ref, kx_ref, sel_ref, vt_ref, o_ref, kcat_ref, qs_ref, acc_ref, sa_ref, sb_ref,
                 *, tq, tsub, tk, past, tiles):
    i = pl.program_id(2)
    nsub = tq // tsub
    chains = [(sb, a) for sb in range(nsub) for a in range(2)]
    sel = sel_ref[0]

    @pl.when(i == 0)
    def _():
        kcat_ref[:, 0:LANES] = k_ref[0]

        def fill(r, carry):
            off = pl.multiple_of(r * tk, tk)
            kcat_ref[pl.ds(off, tk), LANES:2 * LANES] = _dot(kx_ref[0, pl.ds(off, tk), :], sel).astype(BF16)
            return carry

        lax.fori_loop(0, kcat_ref.shape[0] // tk, fill, 0)

    lane = lax.broadcasted_iota(jnp.int32, (tsub, LANES), 1)
    nct = len(chains)
    for h in range(tiles):
        for sb in range(nsub):
            rows = slice(tq * h + tsub * sb, tq * h + tsub * (sb + 1))
            q2 = q_ref[0, rows, :].astype(F32)
            qx2 = _dot(qx_ref[0, rows, :], sel)
            zero = jnp.zeros_like(q2)
            for a, keep_q, keep_x in ((0, lane < HEAD_DIM, lane < Q_EXTRA_SPLIT),
                                      (1, lane >= HEAD_DIM, lane >= Q_EXTRA_SPLIT)):
                c = nct * h + 2 * sb + a
                qs_ref[c, 0:LANES, :] = jnp.where(keep_q, q2, zero).T.astype(BF16)
                qs_ref[c, LANES:2 * LANES, :] = jnp.where(keep_x, qx2, zero).T.astype(BF16)
    acc_ref[...] = jnp.zeros(acc_ref.shape, F32)

    def qk_stage(j, s_ref, h, masked=False):
        off = pl.multiple_of(j * tk, tk)
        kc = kcat_ref[pl.ds(off, tk), :]
        q_lo = past + (i * tiles + h) * tq
        mx = []
        for c, (sb, a) in enumerate(chains):
            s = _dot(kc, qs_ref[nct * h + c])
            if masked:
                kpos = off + lax.broadcasted_iota(jnp.int32, (tk, tsub), 0)
                qpos = q_lo + tsub * sb + lax.broadcasted_iota(jnp.int32, (tk, tsub), 1)
                s = jnp.where(kpos <= qpos, s, NEG_BIG)
            s_ref[c] = s
            mx.append(jnp.max(s, axis=0, keepdims=True))
        return tuple(mx)

    def sm_stage(j, s_ref, mx, state, h):
        off = pl.multiple_of(j * tk, tk)
        vts = [vt_ref[0, HEAD_DIM * a:HEAD_DIM * (a + 1), pl.ds(off, tk)] for a in range(2)]
        out, probs = [], []
        for c, (sb, a) in enumerate(chains):
            m_prev, l_prev = state[c]
            m_new = jnp.maximum(m_prev, mx[c])
            alpha = jnp.exp2(m_prev - m_new)
            p = jnp.exp2(s_ref[c] - m_new)
            out.append((m_new, alpha * l_prev + jnp.sum(p, axis=0, keepdims=True)))
            probs.append((alpha, p.astype(BF16)))
        for c, (sb, a) in enumerate(chains):
            alpha, p = probs[c]
            acc_ref[nct * h + c] = alpha * acc_ref[nct * h + c] + _dot(vts[a], p)
        return tuple(out)

    def finalize(state, h):
        for sb in range(nsub):
            ot = jnp.concatenate([acc_ref[nct * h + 2 * sb + a] * (1.0 / state[2 * sb + a][1]) for a in range(2)],
                                 axis=0)
            rows = slice(tq * h + tsub * sb, tq * h + tsub * (sb + 1))
            o_ref[0, rows, :] = ot.T.astype(BF16)

    def pair_loop(n_pairs, first, h, carry):
        def body(u, carry):
            jprev, mx_a, state = carry
            mx_b = qk_stage(first + 2 * u, sb_ref, h)
            state = sm_stage(jprev, sa_ref, mx_a, state, h)
            mx_a = qk_stage(first + 2 * u + 1, sa_ref, h)
            state = sm_stage(first + 2 * u, sb_ref, mx_b, state, h)
            return first + 2 * u + 1, mx_a, state
        return lax.fori_loop(0, n_pairs, body, carry)

    init = tuple((jnp.full((1, tsub), NEG_BIG, F32), jnp.zeros((1, tsub), F32)) for _ in chains)
    if tiles == 1:
        jm = (past + i * tq) // tk
        mx0 = qk_stage(jm, sa_ref, 0, masked=True)
        carry = pair_loop(jm // 2, 0, 0, (jm, mx0, init))

        def odd_tail(args):
            jprev, mx_a, state = args
            mx_b = qk_stage(jm - 1, sb_ref, 0)
            state = sm_stage(jprev, sa_ref, mx_a, state, 0)
            return sm_stage(jm - 1, sb_ref, mx_b, state, 0)

        def even_tail(args):
            jprev, mx_a, state = args
            return sm_stage(jprev, sa_ref, mx_a, state, 0)

        finalize(lax.cond(jm % 2 == 1, odd_tail, even_tail, carry), 0)
    else:
        ja, jb = 2 * i, 2 * i + 1
        mx0 = qk_stage(ja, sa_ref, 0, masked=True)
        jprev, mx_a, st_a = pair_loop(i, 0, 0, (ja, mx0, init))
        mx_b = qk_stage(jb, sb_ref, 1, masked=True)
        st_a = sm_stage(jprev, sa_ref, mx_a, st_a, 0)
        mx_a = qk_stage(0, sa_ref, 1)
        st_b = sm_stage(jb, sb_ref, mx_b, init, 1)
        finalize(st_a, 0)
        jprev, mx_a, st_b = pair_loop(i, 1, 1, (0, mx_a, st_b))
        finalize(sm_stage(jprev, sa_ref, mx_a, st_b, 1), 1)


def _attention(q, qx, k, kx, vt, *, past, tq, tk):
    b, t, _ = q.shape
    tkeys = k.shape[1]
    npairs = N_HEADS // 2
    assert t % tq == 0 and tkeys % tk == 0 and past % tq == 0 and tk % tq == 0
    tsub = min(tq, MXU_DIM)
    assert tq % tsub == 0
    tiles = 2 if (past == 0 and tq == tk and t % (2 * tq) == 0) else 1
    rows = tiles * tq
    nchain = 2 * (tq // tsub)
    qoff = past // rows
    assert past % rows == 0
    nbuf = pl.Buffered(1 if t // rows > 1 else 2)
    return pl.pallas_call(
        functools.partial(_attn_kernel, tq=tq, tsub=tsub, tk=tk, past=past, tiles=tiles),
        grid=(b, npairs, t // rows),
        in_specs=[pl.BlockSpec((1, rows, LANES), lambda bi, p, i: (bi, i, p)),
                  pl.BlockSpec((1, rows, LANES), lambda bi, p, i: (bi, qoff + i, 0)),
                  pl.BlockSpec((1, tkeys, LANES), lambda bi, p, i: (bi, 0, p), pipeline_mode=nbuf),
                  pl.BlockSpec((1, tkeys, LANES), lambda bi, p, i: (bi, 0, 0), pipeline_mode=nbuf),
                  pl.BlockSpec((1, LANES, LANES), lambda bi, p, i: (p, 0, 0)),
                  pl.BlockSpec((1, LANES, tkeys), lambda bi, p, i: (bi, p, 0), pipeline_mode=nbuf)],
        out_specs=pl.BlockSpec((1, rows, LANES), lambda bi, p, i: (bi, i, p)),
        out_shape=jax.ShapeDtypeStruct((b, t, FOX_DIM), BF16),
        scratch_shapes=[pltpu.VMEM((tkeys, 2 * LANES), BF16),
                        pltpu.VMEM((tiles * nchain, 2 * LANES, tsub), BF16),
                        pltpu.VMEM((tiles * nchain, HEAD_DIM, tsub), F32),
                        pltpu.VMEM((nchain, tk, tsub), F32),
                        pltpu.VMEM((nchain, tk, tsub), F32)],
        compiler_params=_params(("arbitrary", "arbitrary", "arbitrary")),
        name="fox_attention",
    )(q, qx, k, kx, _pair_selectors(), vt)


def _ssd_kernel(xbc_ref, z_ref, dt_ref, cprev_ref, sprev_ref, wconv_ref, bconv_ref, alog_ref,
                dskip_ref, gssd_ref, rexp_ref,
                y_ref, hfin_ref, ext_ref, state_ref, *, L):
    c = pl.program_id(1)

    @pl.when(c == 0)
    def _():
        state_ref[...] = sprev_ref[0]
        ext_ref[...] = cprev_ref[0]

    x_tile = xbc_ref[0]
    ext = jnp.concatenate([ext_ref[...], x_tile], axis=0)
    w = wconv_ref[...]
    pre = bconv_ref[...] + x_tile * w[CONV_W - 1:CONV_W, :]
    for back in range(1, CONV_W):
        shifted = pltpu.roll(ext, back, axis=0)[SUBLANES:, :]
        pre = pre + shifted * w[CONV_W - 1 - back:CONV_W - back, :]
    ext_ref[...] = x_tile[L - SUBLANES:L, :]
    u = pre * _sigmoid(pre)
    xs = u[:, 0:SSD_DIM]
    bmat = u[:, SSD_DIM:SSD_DIM + SSD_GROUPS * SSD_STATE].astype(BF16)
    cmat = u[:, SSD_DIM + SSD_GROUPS * SSD_STATE:].astype(BF16)

    dt = dt_ref[0]
    a = dt * (-jnp.exp(alog_ref[...]))
    r = lax.broadcasted_iota(jnp.int32, (L, L), 0)
    cc = lax.broadcasted_iota(jnp.int32, (L, L), 1)
    causal = r >= cc
    ltri = jnp.where(causal, 1.0, 0.0).astype(BF16)
    cum = _sel_dot(ltri, a)
    e_r = lax.broadcasted_iota(jnp.int32, (N_HEADS, N_HEADS), 0)
    e_c = lax.broadcasted_iota(jnp.int32, (N_HEADS, N_HEADS), 1)
    eye = jnp.where(e_r == e_c, 1.0, 0.0).astype(BF16)
    cum_t = _sel_dot_nt(eye, cum)
    last = cum[L - 1:L, :]

    dt_t = _sel_dot_nt(eye, dt)

    rexp = rexp_ref[...]
    wendx = _dot_sel2(dt * jnp.exp(last - cum), rexp)
    ecumx = _dot_sel2(jnp.exp(cum), rexp)
    xs_b = xs.astype(BF16)
    xw_b = (xs * wendx).astype(BF16)
    elast_b = jnp.broadcast_to(jnp.exp(cum_t[:, L - 1:L]), (N_HEADS, SSD_STATE))
    lane = lax.broadcasted_iota(jnp.int32, (L, LANES), 1)

    heads_per_group = N_HEADS // SSD_GROUPS
    ys = []
    for g in range(SSD_GROUPS):
        gs = slice(GROUP_DIM * g, GROUP_DIM * (g + 1))
        cg = cmat[:, SSD_STATE * g:SSD_STATE * (g + 1)]
        bg = bmat[:, SSD_STATE * g:SSD_STATE * (g + 1)]
        cb = _dot_nt(cg, bg)
        st = state_ref[gs, :]
        yoff = _dot_nt(cg, st.astype(BF16))
        for pr in range(heads_per_group // 2):
            ps = slice(GROUP_DIM * g + LANES * pr, GROUP_DIM * g + LANES * (pr + 1))
            slab = xs_b[:, ps]
            res = []
            for hh in range(2):
                hd = heads_per_group * g + 2 * pr + hh
                seg = cum[:, hd:hd + 1] - cum_t[hd:hd + 1, :]
                dec = jnp.exp(jnp.where(causal, seg, -jnp.inf))
                res.append(_dot((cb * dec * dt_t[hd:hd + 1, :]).astype(BF16), slab))
            ydiag = jnp.where(lane < HEAD_DIM, res[0], res[1])
            ys.append(ydiag + yoff[:, LANES * pr:LANES * (pr + 1)] * ecumx[:, ps])
        snew = _dot_tn(xw_b[:, gs], bg)
        for hh in range(heads_per_group):
            hd = heads_per_group * g + hh
            rows = slice(HEAD_DIM * hd, HEAD_DIM * (hd + 1))
            state_ref[rows, :] = (elast_b[hd:hd + 1, :] * st[HEAD_DIM * hh:HEAD_DIM * (hh + 1), :]
                                  + snew[HEAD_DIM * hh:HEAD_DIM * (hh + 1), :])

    y = jnp.concatenate(ys, axis=-1) + dskip_ref[...] * xs
    zt = z_ref[0]
    ug = y * (zt * _sigmoid(zt))
    outs = []
    for g in range(SSD_GROUPS):
        ugg = ug[:, GROUP_DIM * g:GROUP_DIM * (g + 1)]
        outs.append(ugg * lax.rsqrt(jnp.mean(ugg * ugg, axis=-1, keepdims=True) + EPS))
    y_ref[0] = (jnp.concatenate(outs, axis=-1) * gssd_ref[...]).astype(BF16)

    @pl.when(c == pl.num_programs(1) - 1)
    def _():
        hfin_ref[0] = state_ref[...]


def _ssd(xbc, z, dt, conv_prev8, ssm_prev, w_conv, b_conv, a_log, d_skip, g_ssd, *, L):
    b, t, _ = xbc.shape
    assert t % L == 0 and L % SUBLANES == 0
    head_of_lane = jnp.arange(SSD_DIM) // HEAD_DIM
    rexp = (jnp.arange(N_HEADS)[:, None] == head_of_lane[None, :]).astype(BF16)
    tile = lambda w: pl.BlockSpec((1, L, w), lambda bi, ci: (bi, ci, 0))
    per_b = lambda s: pl.BlockSpec((1,) + s, lambda bi, ci: (bi, 0, 0))
    return pl.pallas_call(
        functools.partial(_ssd_kernel, L=L),
        grid=(b, t // L),
        in_specs=[tile(CONV_DIM), tile(SSD_DIM), tile(N_HEADS),
                  per_b((SUBLANES, CONV_DIM)), per_b((SSD_DIM, SSD_STATE)),
                  _resident((CONV_W, CONV_DIM)), _resident((1, CONV_DIM)), _resident((1, N_HEADS)),
                  _resident((1, SSD_DIM)), _resident((1, SSD_DIM)),
                  _resident((N_HEADS, SSD_DIM))],
        out_specs=(tile(SSD_DIM), per_b((SSD_DIM, SSD_STATE))),
        out_shape=(jax.ShapeDtypeStruct((b, t, SSD_DIM), BF16),
                   jax.ShapeDtypeStruct((b, SSD_DIM, SSD_STATE), F32)),
        scratch_shapes=[pltpu.VMEM((SUBLANES, CONV_DIM), F32),
                        pltpu.VMEM((SSD_DIM, SSD_STATE), F32)],
        compiler_params=_params(("arbitrary", "arbitrary")),
        name="ssd_scan",
    )(xbc, z, dt, conv_prev8, ssm_prev, w_conv, b_conv.reshape(1, CONV_DIM), a_log.reshape(1, N_HEADS),
      jnp.repeat(d_skip, HEAD_DIM).reshape(1, SSD_DIM), g_ssd.reshape(1, SSD_DIM), rexp)


def _ffn_kernel(x_ref, attn_ref, y_ref, wo_ref, g_ref, wg_ref, wu_ref, wd_ref, o_ref, hf_ref):
    j = pl.program_id(1)

    @pl.when(j == 0)
    def _():
        mix = _dot(attn_ref[...], wo_ref[0:FOX_DIM, :]) + _dot(y_ref[...], wo_ref[FOX_DIM:, :])
        x = x_ref[...] + mix
        ms = jnp.mean(x * x, axis=-1, keepdims=True)
        hf_ref[...] = (x * lax.rsqrt(ms + EPS) * g_ref[...]).astype(BF16)
        o_ref[...] = x

    hf = hf_ref[...]
    gate = _dot(hf, wg_ref[...])
    act = (gate * _sigmoid(gate) * _dot(hf, wu_ref[...])).astype(BF16)
    o_ref[...] += _dot(act, wd_ref[...])


def _outproj_ffn(x2d, attn, y, w_out, g_ffn, w_gate, w_up, w_down):
    n, d = x2d.shape
    dff = w_gate.shape[1]
    tm, tf = min(FFN_TOKENS, n), FFN_FF_TILE
    assert n % tm == 0 and dff % tf == 0
    row = lambda w: pl.BlockSpec((tm, w), lambda i, j: (i, 0))
    return pl.pallas_call(
        _ffn_kernel,
        grid=(n // tm, dff // tf),
        in_specs=[row(d), row(FOX_DIM), row(SSD_DIM), _resident((FOX_DIM + SSD_DIM, d)),
                  _resident((1, d)),
                  pl.BlockSpec((d, tf), lambda i, j: (0, j)),
                  pl.BlockSpec((d, tf), lambda i, j: (0, j)),
                  pl.BlockSpec((tf, d), lambda i, j: (j, 0))],
        out_specs=pl.BlockSpec((tm, d), lambda i, j: (i, 0)),
        out_shape=jax.ShapeDtypeStruct((n, d), F32),
        scratch_shapes=[pltpu.VMEM((tm, d), BF16)],
        compiler_params=_params(("arbitrary", "arbitrary")),
        name="ffn",
    )(x2d, attn, y, w_out, g_ffn.reshape(1, d), w_gate, w_up, w_down)


def _pack_w_in(w_in):
    d = w_in.shape[0]
    o_f = 3 * FOX_DIM
    o_z = o_f + N_HEADS
    o_xbc = o_z + SSD_DIM
    o_dt = o_xbc + CONV_DIM
    pad = jnp.zeros((d, LANES - N_HEADS), w_in.dtype)
    return jnp.concatenate([w_in[:, :o_f], w_in[:, o_z:o_xbc], w_in[:, o_xbc:o_dt],
                            w_in[:, o_f:o_z], pad, w_in[:, o_dt:], pad], axis=1).astype(BF16)


def _tile_plan(t, past):
    if past == 0:
        tile = min(4 * MXU_DIM, t)
        return dict(attn_tq=tile, attn_tk=tile, fbias_blk=min(MXU_DIM, t), ssd_chunk=min(LANES, t))
    return dict(attn_tq=t, attn_tk=LANES, fbias_blk=LANES, ssd_chunk=t)


def _layer(x, conv_prev, ssm_prev, k_prev, v_prev, logf_prev, wts):
    (g_mix, w_all, f_bias, g_q, g_k, w_conv, b_conv, dt_bias, a_log, d_skip, g_ssd,
     w_out, g_ffn, w_gate, w_up, w_down) = wts
    b, t, d = x.shape
    n = b * t
    x2d = x.reshape(n, d)
    qb, k, kb, v, vb, z, xbc, logf, dt = _inproj(x2d, t, g_mix, w_all, g_q, g_k, f_bias, dt_bias)

    past = 0 if k_prev is None else k_prev.shape[1]
    plan = _tile_plan(t, past)
    kb3, logf3 = kb.reshape(b, t, FOX_DIM), logf.reshape(b, t, N_HEADS)
    vt3 = vb if vb.ndim == 3 else jnp.transpose(vb.reshape(b, t, FOX_DIM), (0, 2, 1))
    logf_all = logf3
    if past:
        kb3 = jnp.concatenate([k_prev.reshape(b, past, FOX_DIM).astype(BF16), kb3], axis=1)
        vt_prev = jnp.transpose(v_prev.reshape(b, past, FOX_DIM).astype(BF16), (0, 2, 1))
        vt3 = jnp.concatenate([vt_prev, vt3], axis=2)
        logf_all = jnp.concatenate([logf_prev.astype(F32), logf3], axis=1)
    tkeys = past + t
    tk = plan["attn_tk"]
    if tkeys % tk:
        tk = -(-tkeys // LANES) * LANES
        grow = ((0, 0), (0, tk - tkeys), (0, 0))
        kb3, logf_all = jnp.pad(kb3, grow), jnp.pad(logf_all, grow)
        vt3 = jnp.pad(vt3, ((0, 0), (0, 0), (0, tk - tkeys)))
    kx, qx = _fbias(logf_all, min(plan["fbias_blk"], tk))
    attn = _attention(qb.reshape(b, t, FOX_DIM), qx, kb3, kx, vt3, past=past, tq=plan["attn_tq"], tk=tk)

    conv_prev8 = jnp.pad(conv_prev.astype(F32), ((0, 0), (SUBLANES - (CONV_W - 1), 0), (0, 0)))
    y, h_last = _ssd(xbc.reshape(b, t, CONV_DIM), z.reshape(b, t, SSD_DIM), dt.reshape(b, t, N_HEADS),
                     conv_prev8, ssm_prev.astype(F32).reshape(b, SSD_DIM, SSD_STATE),
                     w_conv, b_conv, a_log, d_skip, g_ssd, L=plan["ssd_chunk"])

    out = _outproj_ffn(x2d, attn.reshape(n, FOX_DIM), y.reshape(n, SSD_DIM), w_out, g_ffn, w_gate, w_up, w_down)

    xbc_all = jnp.concatenate([conv_prev.astype(F32), xbc.reshape(b, t, CONV_DIM)], axis=1)
    return (out.reshape(b, t, d), xbc_all[:, -(CONV_W - 1):],
            h_last.reshape(b, N_HEADS, HEAD_DIM, SSD_STATE),
            k.reshape(b, t, N_HEADS, HEAD_DIM), v.reshape(b, t, N_HEADS, HEAD_DIM), logf3)


def kernel(x_prompt, x_sample, cache_conv, state_ssm, cache_fox_k, cache_fox_v, cache_fox_logf, g_mix, w_in, f_bias, g_q, g_k, w_conv, b_conv, dt_bias, a_log, d_skip, g_ssd, w_out, g_ffn, w_gate, w_up, w_down):
    depth = g_mix.shape[0]
    yp, ys = x_prompt, x_sample
    bp = x_prompt.shape[0]
    outs_p = [[] for _ in range(5)]
    outs_s = [[] for _ in range(5)]
    for i in range(depth):
        wts = (g_mix[i], _pack_w_in(w_in[i]), f_bias[i], g_q[i], g_k[i], w_conv[i], b_conv[i], dt_bias[i],
               a_log[i], d_skip[i], g_ssd[i], w_out[i].astype(BF16), g_ffn[i],
               w_gate[i].astype(BF16), w_up[i].astype(BF16), w_down[i].astype(BF16))
        yp, *rest = _layer(
            yp, jnp.zeros((bp, CONV_W - 1, CONV_DIM), F32),
            jnp.zeros((bp, N_HEADS, HEAD_DIM, SSD_STATE), F32), None, None, None, wts)
        for lst, val in zip(outs_p, rest):
            lst.append(val)
        ys, *rest = _layer(
            ys, cache_conv[i], state_ssm[i], cache_fox_k[i], cache_fox_v[i], cache_fox_logf[i], wts)
        for lst, val in zip(outs_s, rest):
            lst.append(val)
    return (yp, ys, *[jnp.stack(l) for l in outs_p], *[jnp.stack(l) for l in outs_s])
```

```python
import functools

import jax
import jax.numpy as jnp
import numpy as np
from jax import lax
from jax.experimental import pallas as pl
from jax.experimental.pallas import tpu as pltpu

F32 = jnp.float32
BF16 = jnp.bfloat16

EPS = 1e-6
HEAD_DIM = 64
N_HEADS = 16
FOX_DIM = 1024
SSD_DIM = 1024
SSD_GROUPS = 2
SSD_STATE = 128
GROUP_DIM = SSD_DIM // SSD_GROUPS
CONV_W = 4
CONV_DIM = SSD_DIM + 2 * SSD_GROUPS * SSD_STATE
LANES = 128
SUBLANES = 8
MXU_DIM = 256
NEG_BIG = -1e30
LOG2E = 1.4426950408889634
VMEM_LIMIT = 56 * 1024 * 1024
INPROJ_TOKENS = MXU_DIM
FFN_TOKENS = 2 * MXU_DIM
FFN_FF_TILE = 2 * MXU_DIM
FBIAS_TIME_TILE = 8 * MXU_DIM

COL_Q, COL_K, COL_V, COL_Z = 0, FOX_DIM, 2 * FOX_DIM, 3 * FOX_DIM
COL_XBC = COL_Z + SSD_DIM
COL_F = COL_XBC + CONV_DIM
COL_DT = COL_F + LANES
IN_COLS = COL_DT + LANES


def _dot(a, b):
    return jnp.dot(a, b, preferred_element_type=F32)


def _dot_nt(a, b):
    return lax.dot_general(a, b, (((1,), (1,)), ((), ())), preferred_element_type=F32)


def _dot_tn(a, b):
    return lax.dot_general(a, b, (((0,), (0,)), ((), ())), preferred_element_type=F32)


def _split3(x):
    x1 = x.astype(BF16)
    r = x - x1.astype(F32)
    x2 = r.astype(BF16)
    r = r - x2.astype(F32)
    return x1, x2, r.astype(BF16)


def _sel_dot(sel, x):
    x1, x2, x3 = _split3(x)
    return _dot(sel, x1) + _dot(sel, x2) + _dot(sel, x3)


def _dot_sel2(x, sel):
    x1 = x.astype(BF16)
    x2 = (x - x1.astype(F32)).astype(BF16)
    return _dot(x1, sel) + _dot(x2, sel)


def _sel_dot_nt(sel, x):
    x1, x2, x3 = _split3(x)
    return _dot_nt(sel, x1) + _dot_nt(sel, x2) + _dot_nt(sel, x3)


def _sigmoid(x):
    return 0.5 * jnp.tanh(0.5 * x) + 0.5


def _softplus(x):
    return jnp.maximum(x, 0.0) + jnp.log1p(jnp.exp(-jnp.abs(x)))


def _resident(shape):
    nd = len(shape)
    return pl.BlockSpec(shape, lambda *_: (0,) * nd, pipeline_mode=pl.Buffered(1))


def _params(sem):
    return pltpu.CompilerParams(dimension_semantics=sem, vmem_limit_bytes=VMEM_LIMIT)


def _inproj_kernel(x_ref, gmix_ref, w_ref, e_ref, gq_ref, gk_ref, fb_ref, dtb_ref,
                   q_ref, k_ref, kb_ref, v_ref, vb_ref, z_ref, xbc_ref, logf_ref, dt_ref, *, v_transposed):
    x = x_ref[...]
    ms = jnp.mean(x * x, axis=-1, keepdims=True)
    h = (x * lax.rsqrt(ms + EPS) * gmix_ref[...]).astype(BF16)
    e = e_ref[...]

    def head_rms(y, g):
        ysq = y * y
        hi = ysq.astype(BF16)
        lo = (ysq - hi.astype(F32)).astype(BF16)
        parts = []
        for c in range(FOX_DIM // MXU_DIM):
            sl = slice(MXU_DIM * c, MXU_DIM * (c + 1))
            parts.append(_dot(hi[:, sl], e) + _dot(lo[:, sl], e))
        ss = jnp.concatenate(parts, axis=-1)
        return y * lax.rsqrt(ss * (1.0 / HEAD_DIM) + EPS) * g

    q = _dot(h, w_ref[:, COL_Q:COL_Q + FOX_DIM])
    q_ref[...] = (head_rms(q, gq_ref[...]) * (HEAD_DIM ** -0.5 * LOG2E)).astype(BF16)
    k = head_rms(_dot(h, w_ref[:, COL_K:COL_K + FOX_DIM]), gk_ref[...])
    for hd in range(N_HEADS):
        k_ref[:, hd, :] = k[:, HEAD_DIM * hd:HEAD_DIM * (hd + 1)]
    kb_ref[...] = k.astype(BF16)
    v = _dot(h, w_ref[:, COL_V:COL_V + FOX_DIM])
    for hd in range(N_HEADS):
        v_ref[:, hd, :] = v[:, HEAD_DIM * hd:HEAD_DIM * (hd + 1)]
    if v_transposed:
        vb_ref[0] = v.T.astype(BF16)
    else:
        vb_ref[...] = v.astype(BF16)
    z_ref[...] = _dot(h, w_ref[:, COL_Z:COL_Z + SSD_DIM])
    xbc_ref[...] = _dot(h, w_ref[:, COL_XBC:COL_XBC + CONV_DIM])
    f_raw = _dot(h, w_ref[:, COL_F:COL_F + LANES])[:, :N_HEADS]
    logf_ref[...] = -_softplus(-(f_raw + fb_ref[...]))
    dt_raw = _dot(h, w_ref[:, COL_DT:COL_DT + LANES])[:, :N_HEADS]
    dt_ref[...] = _softplus(dt_raw + dtb_ref[...])


def _inproj(x2d, seq, g_mix, w_all, g_q, g_k, f_bias, dt_bias):
    n, d = x2d.shape
    tm = min(INPROJ_TOKENS, n)
    assert n % tm == 0 and n % seq == 0
    v_transposed = seq % tm == 0
    if v_transposed:
        tiles = seq // tm
        vb_shape = jax.ShapeDtypeStruct((n // seq, FOX_DIM, seq), BF16)
        vb_spec = pl.BlockSpec((1, FOX_DIM, tm), lambda i: (i // tiles, 0, i % tiles))
    else:
        vb_shape = jax.ShapeDtypeStruct((n, FOX_DIM), BF16)
        vb_spec = pl.BlockSpec((tm, FOX_DIM), lambda i: (i, 0))
    blk = jnp.arange(MXU_DIM) // HEAD_DIM
    e = (blk[:, None] == blk[None, :]).astype(BF16)
    row = lambda w: pl.BlockSpec((tm, w), lambda i: (i, 0))
    heads = pl.BlockSpec((tm, N_HEADS, HEAD_DIM), lambda i: (i, 0, 0))
    out_shapes = (
        jax.ShapeDtypeStruct((n, FOX_DIM), BF16),
        jax.ShapeDtypeStruct((n, N_HEADS, HEAD_DIM), F32),
        jax.ShapeDtypeStruct((n, FOX_DIM), BF16),
        jax.ShapeDtypeStruct((n, N_HEADS, HEAD_DIM), F32),
        vb_shape,
        jax.ShapeDtypeStruct((n, SSD_DIM), F32),
        jax.ShapeDtypeStruct((n, CONV_DIM), F32),
        jax.ShapeDtypeStruct((n, N_HEADS), F32),
        jax.ShapeDtypeStruct((n, N_HEADS), F32),
    )
    return pl.pallas_call(
        functools.partial(_inproj_kernel, v_transposed=v_transposed),
        grid=(n // tm,),
        in_specs=[row(d), _resident((1, d)), _resident((d, IN_COLS)), _resident((MXU_DIM, MXU_DIM)),
                  _resident((1, FOX_DIM)), _resident((1, FOX_DIM)),
                  _resident((1, N_HEADS)), _resident((1, N_HEADS))],
        out_specs=(row(FOX_DIM), heads, row(FOX_DIM), heads, vb_spec,
                   row(SSD_DIM), row(CONV_DIM), row(N_HEADS), row(N_HEADS)),
        out_shape=out_shapes,
        compiler_params=_params(("arbitrary",)),
        name="inproj",
    )(x2d, g_mix.reshape(1, d), w_all, e,
      jnp.tile(g_q, N_HEADS).reshape(1, FOX_DIM), jnp.tile(g_k, N_HEADS).reshape(1, FOX_DIM),
      f_bias.reshape(1, N_HEADS), dt_bias.reshape(1, N_HEADS))


N_EXTRA = 3
Q_EXTRA_SPLIT = 2 * N_EXTRA
PAIR_EXTRA = 16


def _placement_constants():
    pq = np.zeros((N_EXTRA, N_HEADS, LANES), np.float32)
    pk = np.zeros((N_EXTRA, N_HEADS, LANES), np.float32)
    oq = np.zeros((1, LANES), np.float32)
    ok = np.zeros((1, LANES), np.float32)
    for h in range(N_HEADS):
        base = PAIR_EXTRA * (h // 2)
        for c in range(N_EXTRA):
            if h % 2 == 0:
                pq[c, h, base + c] = 1.0
                pk[c, h, base + N_EXTRA + c] = -1.0
                oq[0, base + N_EXTRA + c] = 1.0
                ok[0, base + c] = 1.0
            else:
                pq[c, h, base + 3 * N_EXTRA + c] = 1.0
                pk[c, h, base + 2 * N_EXTRA + c] = -1.0
                oq[0, base + 2 * N_EXTRA + c] = 1.0
                ok[0, base + 3 * N_EXTRA + c] = 1.0
    return (jnp.asarray(pq, BF16), jnp.asarray(pk, BF16), jnp.asarray(oq), jnp.asarray(ok))


def _fbias_kernel(logf_ref, pq_ref, pk_ref, oq_ref, ok_ref, kx_ref, qx_ref, carry_ref, *, blk, nblk):
    r = lax.broadcasted_iota(jnp.int32, (blk, blk), 0)
    c = lax.broadcasted_iota(jnp.int32, (blk, blk), 1)
    ltri = jnp.where(r >= c, 1.0, 0.0).astype(BF16)

    @pl.when(pl.program_id(1) == 0)
    def _():
        carry_ref[...] = jnp.zeros(carry_ref.shape, F32)

    lane = lax.broadcasted_iota(jnp.int32, (blk, N_EXTRA * N_HEADS), 1)

    def body(i, carry):
        off = pl.multiple_of(i * blk, blk)
        f = _sel_dot(ltri, logf_ref[0, pl.ds(off, blk), :]) + carry
        p1, p2, p3 = _split3(f * LOG2E)
        pieces = jnp.where(lane < N_HEADS, p1, jnp.where(lane < 2 * N_HEADS, p2, p3))
        kx_ref[0, pl.ds(off, blk), :] = (ok_ref[...] + _dot(pieces, pk_ref[...])).astype(BF16)
        qx_ref[0, pl.ds(off, blk), :] = (oq_ref[...] + _dot(pieces, pq_ref[...])).astype(BF16)
        return f[blk - 1:blk, :]

    carry_ref[...] = lax.fori_loop(0, nblk, body, carry_ref[...], unroll=True)


def _fbias(logf3, blk):
    b, t, _ = logf3.shape
    tt = t if t <= 2 * FBIAS_TIME_TILE else FBIAS_TIME_TILE
    assert t % tt == 0 and tt % blk == 0
    pq, pk, oq, ok = _placement_constants()
    width = N_EXTRA * N_HEADS
    out = jax.ShapeDtypeStruct((b, t, LANES), BF16)
    tile = lambda w: pl.BlockSpec((1, tt, w), lambda bi, ti: (bi, ti, 0))
    return pl.pallas_call(
        functools.partial(_fbias_kernel, blk=blk, nblk=tt // blk),
        grid=(b, t // tt),
        in_specs=[tile(width), _resident((width, LANES)), _resident((width, LANES)),
                  _resident((1, LANES)), _resident((1, LANES))],
        out_specs=(tile(LANES), tile(LANES)),
        out_shape=(out, out),
        scratch_shapes=[pltpu.VMEM((1, width), F32)],
        compiler_params=_params(("arbitrary", "arbitrary")),
        name="fbias",
    )(jnp.tile(logf3, (1, 1, N_EXTRA)), pq.reshape(width, LANES), pk.reshape(width, LANES), oq, ok)


def _pair_selectors():
    sel = np.zeros((N_HEADS // 2, LANES, LANES), np.float32)
    for p in range(N_HEADS // 2):
        for e in range(PAIR_EXTRA):
            sel[p, PAIR_EXTRA * p + e, e] = 1.0
    return jnp.asarray(sel, BF16)


def _attn_kernel(q_ref, qx_ref, k_ref, kx_ref, sel_ref, v_ref, *rest, tq, tsub, tk, past, tiles):
    if past:
        kc_ref, vc_ref, o_ref, kcat_ref, qs_ref, acc_ref, sa_ref, sb_ref, vcat_ref = rest
    else:
        o_ref, kcat_ref, qs_ref, acc_ref, sa_ref, sb_ref = rest
    i = pl.program_id(2)
    nsub = tq // tsub
    chains = [(sb, a) for sb in range(nsub) for a in range(2)]
    sel = sel_ref[0]

    @pl.when(i == 0)
    def _():
        if past:
            new = k_ref.shape[1]
            for dst, cache, fresh in ((kcat_ref, kc_ref, k_ref), (vcat_ref, vc_ref, v_ref)):
                dst[0:past, 0:LANES] = cache[0].astype(BF16)
                dst[past:past + new, 0:LANES] = fresh[0]
                if dst.shape[0] > past + new:
                    dst[past + new:, 0:LANES] = jnp.zeros((dst.shape[0] - past - new, LANES), BF16)
        else:
            kcat_ref[:, 0:LANES] = k_ref[0]

        def fill(r, carry):
            off = pl.multiple_of(r * tk, tk)
            kcat_ref[pl.ds(off, tk), LANES:2 * LANES] = _dot(kx_ref[0, pl.ds(off, tk), :], sel).astype(BF16)
            return carry

        lax.fori_loop(0, kcat_ref.shape[0] // tk, fill, 0)

    lane = lax.broadcasted_iota(jnp.int32, (tsub, LANES), 1)
    nct = len(chains)
    for h in range(tiles):
        for sb in range(nsub):
            rows = slice(tq * h + tsub * sb, tq * h + tsub * (sb + 1))
            q2 = q_ref[0, rows, :].astype(F32)
            qx2 = _dot(qx_ref[0, rows, :], sel)
            zero = jnp.zeros_like(q2)
            for a, keep_q, keep_x in ((0, lane < HEAD_DIM, lane < Q_EXTRA_SPLIT),
                                      (1, lane >= HEAD_DIM, lane >= Q_EXTRA_SPLIT)):
                c = nct * h + 2 * sb + a
                qs_ref[c, 0:LANES, :] = jnp.where(keep_q, q2, zero).T.astype(BF16)
                qs_ref[c, LANES:2 * LANES, :] = jnp.where(keep_x, qx2, zero).T.astype(BF16)
    acc_ref[...] = jnp.zeros(acc_ref.shape, F32)

    def qk_stage(j, s_ref, h, masked=False):
        off = pl.multiple_of(j * tk, tk)
        kc = kcat_ref[pl.ds(off, tk), :]
        q_lo = past + (i * tiles + h) * tq
        mx = []
        for c, (sb, a) in enumerate(chains):
            s = _dot(kc, qs_ref[nct * h + c])
            if masked:
                kpos = off + lax.broadcasted_iota(jnp.int32, (tk, tsub), 0)
                qpos = q_lo + tsub * sb + lax.broadcasted_iota(jnp.int32, (tk, tsub), 1)
                s = jnp.where(kpos <= qpos, s, NEG_BIG)
            s_ref[c] = s
            mx.append(jnp.max(s, axis=0, keepdims=True))
        return tuple(mx)

    def sm_stage(j, s_ref, mx, state, h):
        off = pl.multiple_of(j * tk, tk)
        if past:
            vall = vcat_ref[pl.ds(off, tk), :]
        else:
            vts = [v_ref[0, HEAD_DIM * a:HEAD_DIM * (a + 1), pl.ds(off, tk)] for a in range(2)]
        out, probs = [], []
        for c, (sb, a) in enumerate(chains):
            m_prev, l_prev = state[c]
            m_new = jnp.maximum(m_prev, mx[c])
            alpha = jnp.exp2(m_prev - m_new)
            p = jnp.exp2(s_ref[c] - m_new)
            out.append((m_new, alpha * l_prev + jnp.sum(p, axis=0, keepdims=True)))
            probs.append((alpha, p.astype(BF16)))
        for c, (sb, a) in enumerate(chains):
            alpha, p = probs[c]
            if past:
                pv = _dot_tn(vall, p)[HEAD_DIM * a:HEAD_DIM * (a + 1), :]
            else:
                pv = _dot(vts[a], p)
            acc_ref[nct * h + c] = alpha * acc_ref[nct * h + c] + pv
        return tuple(out)

    def finalize(state, h):
        for sb in range(nsub):
            ot = jnp.concatenate([acc_ref[nct * h + 2 * sb + a] * (1.0 / state[2 * sb + a][1]) for a in range(2)],
                                 axis=0)
            rows = slice(tq * h + tsub * sb, tq * h + tsub * (sb + 1))
            o_ref[0, rows, :] = ot.T.astype(BF16)

    def pair_loop(n_pairs, first, h, carry):
        def body(u, carry):
            jprev, mx_a, state = carry
            mx_b = qk_stage(first + 2 * u, sb_ref, h)
            state = sm_stage(jprev, sa_ref, mx_a, state, h)
            mx_a = qk_stage(first + 2 * u + 1, sa_ref, h)
            state = sm_stage(first + 2 * u, sb_ref, mx_b, state, h)
            return first + 2 * u + 1, mx_a, state
        return lax.fori_loop(0, n_pairs, body, carry)

    init = tuple((jnp.full((1, tsub), NEG_BIG, F32), jnp.zeros((1, tsub), F32)) for _ in chains)
    if tiles == 1:
        jm = (past + i * tq) // tk
        mx0 = qk_stage(jm, sa_ref, 0, masked=True)
        carry = pair_loop(jm // 2, 0, 0, (jm, mx0, init))

        def odd_tail(args):
            jprev, mx_a, state = args
            mx_b = qk_stage(jm - 1, sb_ref, 0)
            state = sm_stage(jprev, sa_ref, mx_a, state, 0)
            return sm_stage(jm - 1, sb_ref, mx_b, state, 0)

        def even_tail(args):
            jprev, mx_a, state = args
            return sm_stage(jprev, sa_ref, mx_a, state, 0)

        finalize(lax.cond(jm % 2 == 1, odd_tail, even_tail, carry), 0)
    else:
        ja, jb = 2 * i, 2 * i + 1
        mx0 = qk_stage(ja, sa_ref, 0, masked=True)
        jprev, mx_a, st_a = pair_loop(i, 0, 0, (ja, mx0, init))
        mx_b = qk_stage(jb, sb_ref, 1, masked=True)
        st_a = sm_stage(jprev, sa_ref, mx_a, st_a, 0)
        mx_a = qk_stage(0, sa_ref, 1)
        st_b = sm_stage(jb, sb_ref, mx_b, init, 1)
        finalize(st_a, 0)
        jprev, mx_a, st_b = pair_loop(i, 1, 1, (0, mx_a, st_b))
        finalize(sm_stage(jprev, sa_ref, mx_a, st_b, 1), 1)


def _attention(q, qx, k, kx, v, cache_k, cache_v, *, tq, tk):
    b, t, _ = q.shape
    past = 0 if cache_k is None else cache_k.shape[1]
    tkeys = kx.shape[1]
    npairs = N_HEADS // 2
    assert t % tq == 0 and tkeys % tk == 0 and past % tq == 0 and tk % tq == 0
    tsub = min(tq, MXU_DIM)
    assert tq % tsub == 0
    tiles = 2 if (past == 0 and tq == tk and t % (2 * tq) == 0) else 1
    rows = tiles * tq
    nchain = 2 * (tq // tsub)
    qoff = past // rows
    assert past % rows == 0 and (past == 0 or t == rows)
    nbuf = pl.Buffered(1 if t // rows > 1 else 2)
    per_pair = lambda n: pl.BlockSpec((1, n, LANES), lambda bi, p, i: (bi, 0, p), pipeline_mode=nbuf)
    in_specs = [pl.BlockSpec((1, rows, LANES), lambda bi, p, i: (bi, i, p)),
                pl.BlockSpec((1, rows, LANES), lambda bi, p, i: (bi, qoff + i, 0)),
                per_pair(k.shape[1]),
                pl.BlockSpec((1, tkeys, LANES), lambda bi, p, i: (bi, 0, 0), pipeline_mode=nbuf),
                pl.BlockSpec((1, LANES, LANES), lambda bi, p, i: (p, 0, 0))]
    scratch = [pltpu.VMEM((tkeys, 2 * LANES), BF16),
               pltpu.VMEM((tiles * nchain, 2 * LANES, tsub), BF16),
               pltpu.VMEM((tiles * nchain, HEAD_DIM, tsub), F32),
               pltpu.VMEM((nchain, tk, tsub), F32),
               pltpu.VMEM((nchain, tk, tsub), F32)]
    if past:
        in_specs += [per_pair(t), per_pair(past), per_pair(past)]
        scratch += [pltpu.VMEM((tkeys, LANES), BF16)]
        operands = (q, qx, k, kx, _pair_selectors(), v, cache_k, cache_v)
    else:
        in_specs += [pl.BlockSpec((1, LANES, tkeys), lambda bi, p, i: (bi, p, 0), pipeline_mode=nbuf)]
        operands = (q, qx, k, kx, _pair_selectors(), v)
    return pl.pallas_call(
        functools.partial(_attn_kernel, tq=tq, tsub=tsub, tk=tk, past=past, tiles=tiles),
        grid=(b, npairs, t // rows),
        in_specs=in_specs,
        out_specs=pl.BlockSpec((1, rows, LANES), lambda bi, p, i: (bi, i, p)),
        out_shape=jax.ShapeDtypeStruct((b, t, FOX_DIM), BF16),
        scratch_shapes=scratch,
        compiler_params=_params(("arbitrary", "arbitrary", "arbitrary")),
        name="fox_attention",
    )(*operands)


def _ssd_kernel(xbc_ref, z_ref, dt_ref, cprev_ref, sprev_ref, wconv_ref, bconv_ref, alog_ref,
                dskip_ref, gssd_ref, rexp_ref,
                y_ref, hfin_ref, ext_ref, state_ref, *, L):
    c = pl.program_id(1)

    @pl.when(c == 0)
    def _():
        state_ref[...] = sprev_ref[0]
        ext_ref[...] = cprev_ref[0]

    x_tile = xbc_ref[0]
    ext = jnp.concatenate([ext_ref[...], x_tile], axis=0)
    w = wconv_ref[...]
    pre = bconv_ref[...] + x_tile * w[CONV_W - 1:CONV_W, :]
    for back in range(1, CONV_W):
        shifted = pltpu.roll(ext, back, axis=0)[SUBLANES:, :]
        pre = pre + shifted * w[CONV_W - 1 - back:CONV_W - back, :]
    ext_ref[...] = x_tile[L - SUBLANES:L, :]
    u = pre * _sigmoid(pre)
    xs = u[:, 0:SSD_DIM]
    bmat = u[:, SSD_DIM:SSD_DIM + SSD_GROUPS * SSD_STATE].astype(BF16)
    cmat = u[:, SSD_DIM + SSD_GROUPS * SSD_STATE:].astype(BF16)

    dt = dt_ref[0]
    a = dt * (-jnp.exp(alog_ref[...]))
    r = lax.broadcasted_iota(jnp.int32, (L, L), 0)
    cc = lax.broadcasted_iota(jnp.int32, (L, L), 1)
    causal = r >= cc
    ltri = jnp.where(causal, 1.0, 0.0).astype(BF16)
    cum = _sel_dot(ltri, a)
    e_r = lax.broadcasted_iota(jnp.int32, (N_HEADS, N_HEADS), 0)
    e_c = lax.broadcasted_iota(jnp.int32, (N_HEADS, N_HEADS), 1)
    eye = jnp.where(e_r == e_c, 1.0, 0.0).astype(BF16)
    cum_t = _sel_dot_nt(eye, cum)
    last = cum[L - 1:L, :]

    dt_t = _sel_dot_nt(eye, dt)

    rexp = rexp_ref[...]
    wendx = _dot_sel2(dt * jnp.exp(last - cum), rexp)
    ecumx = _dot_sel2(jnp.exp(cum), rexp)
    xs_b = xs.astype(BF16)
    xw_b = (xs * wendx).astype(BF16)
    elast_b = jnp.broadcast_to(jnp.exp(cum_t[:, L - 1:L]), (N_HEADS, SSD_STATE))
    lane = lax.broadcasted_iota(jnp.int32, (L, LANES), 1)

    heads_per_group = N_HEADS // SSD_GROUPS
    ys = []
    for g in range(SSD_GROUPS):
        gs = slice(GROUP_DIM * g, GROUP_DIM * (g + 1))
        cg = cmat[:, SSD_STATE * g:SSD_STATE * (g + 1)]
        bg = bmat[:, SSD_STATE * g:SSD_STATE * (g + 1)]
        cb = _dot_nt(cg, bg)
        st = state_ref[gs, :]
        yoff = _dot_nt(cg, st.astype(BF16))
        for pr in range(heads_per_group // 2):
            ps = slice(GROUP_DIM * g + LANES * pr, GROUP_DIM * g + LANES * (pr + 1))
            slab = xs_b[:, ps]
            res = []
            for hh in range(2):
                hd = heads_per_group * g + 2 * pr + hh
                seg = cum[:, hd:hd + 1] - cum_t[hd:hd + 1, :]
                dec = jnp.exp(jnp.where(causal, seg, -jnp.inf))
                res.append(_dot((cb * dec * dt_t[hd:hd + 1, :]).astype(BF16), slab))
            ydiag = jnp.where(lane < HEAD_DIM, res[0], res[1])
            ys.append(ydiag + yoff[:, LANES * pr:LANES * (pr + 1)] * ecumx[:, ps])
        snew = _dot_tn(xw_b[:, gs], bg)
        for hh in range(heads_per_group):
            hd = heads_per_group * g + hh
            rows = slice(HEAD_DIM * hd, HEAD_DIM * (hd + 1))
            state_ref[rows, :] = (elast_b[hd:hd + 1, :] * st[HEAD_DIM * hh:HEAD_DIM * (hh + 1), :]
                                  + snew[HEAD_DIM * hh:HEAD_DIM * (hh + 1), :])

    y = jnp.concatenate(ys, axis=-1) + dskip_ref[...] * xs
    zt = z_ref[0]
    ug = y * (zt * _sigmoid(zt))
    outs = []
    for g in range(SSD_GROUPS):
        ugg = ug[:, GROUP_DIM * g:GROUP_DIM * (g + 1)]
        outs.append(ugg * lax.rsqrt(jnp.mean(ugg * ugg, axis=-1, keepdims=True) + EPS))
    y_ref[0] = (jnp.concatenate(outs, axis=-1) * gssd_ref[...]).astype(BF16)

    @pl.when(c == pl.num_programs(1) - 1)
    def _():
        hfin_ref[0] = state_ref[...]


def _ssd(xbc, z, dt, conv_prev8, ssm_prev, w_conv, b_conv, a_log, d_skip, g_ssd, *, L):
    b, t, _ = xbc.shape
    assert t % L == 0 and L % SUBLANES == 0
    head_of_lane = jnp.arange(SSD_DIM) // HEAD_DIM
    rexp = (jnp.arange(N_HEADS)[:, None] == head_of_lane[None, :]).astype(BF16)
    tile = lambda w: pl.BlockSpec((1, L, w), lambda bi, ci: (bi, ci, 0))
    per_b = lambda s: pl.BlockSpec((1,) + s, lambda bi, ci: (bi, 0, 0))
    return pl.pallas_call(
        functools.partial(_ssd_kernel, L=L),
        grid=(b, t // L),
        in_specs=[tile(CONV_DIM), tile(SSD_DIM), tile(N_HEADS),
                  per_b((SUBLANES, CONV_DIM)), per_b((SSD_DIM, SSD_STATE)),
                  _resident((CONV_W, CONV_DIM)), _resident((1, CONV_DIM)), _resident((1, N_HEADS)),
                  _resident((1, SSD_DIM)), _resident((1, SSD_DIM)),
                  _resident((N_HEADS, SSD_DIM))],
        out_specs=(tile(SSD_DIM), per_b((SSD_DIM, SSD_STATE))),
        out_shape=(jax.ShapeDtypeStruct((b, t, SSD_DIM), BF16),
                   jax.ShapeDtypeStruct((b, SSD_DIM, SSD_STATE), F32)),
        scratch_shapes=[pltpu.VMEM((SUBLANES, CONV_DIM), F32),
                        pltpu.VMEM((SSD_DIM, SSD_STATE), F32)],
        compiler_params=_params(("arbitrary", "arbitrary")),
        name="ssd_scan",
    )(xbc, z, dt, conv_prev8, ssm_prev, w_conv, b_conv.reshape(1, CONV_DIM), a_log.reshape(1, N_HEADS),
      jnp.repeat(d_skip, HEAD_DIM).reshape(1, SSD_DIM), g_ssd.reshape(1, SSD_DIM), rexp)


def _ffn_kernel(x_ref, attn_ref, y_ref, wo_ref, g_ref, wg_ref, wu_ref, wd_ref, o_ref, hf_ref):
    j = pl.program_id(1)

    @pl.when(j == 0)
    def _():
        mix = _dot(attn_ref[...], wo_ref[0:FOX_DIM, :]) + _dot(y_ref[...], wo_ref[FOX_DIM:, :])
        x = x_ref[...] + mix
        ms = jnp.mean(x * x, axis=-1, keepdims=True)
        hf_ref[...] = (x * lax.rsqrt(ms + EPS) * g_ref[...]).astype(BF16)
        o_ref[...] = x

    hf = hf_ref[...]
    gate = _dot(hf, wg_ref[...])
    act = (gate * _sigmoid(gate) * _dot(hf, wu_ref[...])).astype(BF16)
    o_ref[...] += _dot(act, wd_ref[...])


def _outproj_ffn(x2d, attn, y, w_out, g_ffn, w_gate, w_up, w_down):
    n, d = x2d.shape
    dff = w_gate.shape[1]
    tm, tf = min(FFN_TOKENS, n), FFN_FF_TILE
    assert n % tm == 0 and dff % tf == 0
    row = lambda w: pl.BlockSpec((tm, w), lambda i, j: (i, 0))
    return pl.pallas_call(
        _ffn_kernel,
        grid=(n // tm, dff // tf),
        in_specs=[row(d), row(FOX_DIM), row(SSD_DIM), _resident((FOX_DIM + SSD_DIM, d)),
                  _resident((1, d)),
                  pl.BlockSpec((d, tf), lambda i, j: (0, j)),
                  pl.BlockSpec((d, tf), lambda i, j: (0, j)),
                  pl.BlockSpec((tf, d), lambda i, j: (j, 0))],
        out_specs=pl.BlockSpec((tm, d), lambda i, j: (i, 0)),
        out_shape=jax.ShapeDtypeStruct((n, d), F32),
        scratch_shapes=[pltpu.VMEM((tm, d), BF16)],
        compiler_params=_params(("arbitrary", "arbitrary")),
        name="ffn",
    )(x2d, attn, y, w_out, g_ffn.reshape(1, d), w_gate, w_up, w_down)


def _pack_w_in(w_in):
    d = w_in.shape[0]
    o_f = 3 * FOX_DIM
    o_z = o_f + N_HEADS
    o_xbc = o_z + SSD_DIM
    o_dt = o_xbc + CONV_DIM
    pad = jnp.zeros((d, LANES - N_HEADS), w_in.dtype)
    return jnp.concatenate([w_in[:, :o_f], w_in[:, o_z:o_xbc], w_in[:, o_xbc:o_dt],
                            w_in[:, o_f:o_z], pad, w_in[:, o_dt:], pad], axis=1).astype(BF16)


def _tile_plan(t, past):
    if past == 0:
        tile = min(4 * MXU_DIM, t)
        return dict(attn_tq=tile, attn_tk=tile, fbias_blk=min(MXU_DIM, t), ssd_chunk=min(LANES, t))
    return dict(attn_tq=t, attn_tk=LANES, fbias_blk=LANES, ssd_chunk=t)


def _layer(x, conv_prev, ssm_prev, k_prev, v_prev, logf_prev, wts):
    (g_mix, w_all, f_bias, g_q, g_k, w_conv, b_conv, dt_bias, a_log, d_skip, g_ssd,
     w_out, g_ffn, w_gate, w_up, w_down) = wts
    b, t, d = x.shape
    n = b * t
    x2d = x.reshape(n, d)
    qb, k, kb, v, vb, z, xbc, logf, dt = _inproj(x2d, t, g_mix, w_all, g_q, g_k, f_bias, dt_bias)

    past = 0 if k_prev is None else k_prev.shape[1]
    plan = _tile_plan(t, past)
    kb3, logf3 = kb.reshape(b, t, FOX_DIM), logf.reshape(b, t, N_HEADS)
    tk = plan["attn_tk"]
    if past:
        tk = -(-(past + t) // LANES) * LANES
        logf_all = jnp.concatenate([logf_prev.astype(F32), logf3], axis=1)
        logf_all = jnp.pad(logf_all, ((0, 0), (0, tk - past - t), (0, 0)))
        values = vb.reshape(b, t, FOX_DIM) if vb.ndim == 2 else jnp.transpose(vb, (0, 2, 1))
        caches = (k_prev.reshape(b, past, FOX_DIM), v_prev.reshape(b, past, FOX_DIM))
    else:
        values = vb if vb.ndim == 3 else jnp.transpose(vb.reshape(b, t, FOX_DIM), (0, 2, 1))
        logf_all, caches = logf3, (None, None)
    kx, qx = _fbias(logf_all, min(plan["fbias_blk"], tk))
    attn = _attention(qb.reshape(b, t, FOX_DIM), qx, kb3, kx, values, *caches, tq=plan["attn_tq"], tk=tk)

    conv_prev8 = jnp.pad(conv_prev.astype(F32), ((0, 0), (SUBLANES - (CONV_W - 1), 0), (0, 0)))
    y, h_last = _ssd(xbc.reshape(b, t, CONV_DIM), z.reshape(b, t, SSD_DIM), dt.reshape(b, t, N_HEADS),
                     conv_prev8, ssm_prev.astype(F32).reshape(b, SSD_DIM, SSD_STATE),
                     w_conv, b_conv, a_log, d_skip, g_ssd, L=plan["ssd_chunk"])

    out = _outproj_ffn(x2d, attn.reshape(n, FOX_DIM), y.reshape(n, SSD_DIM), w_out, g_ffn, w_gate, w_up, w_down)

    xbc_all = jnp.concatenate([conv_prev.astype(F32), xbc.reshape(b, t, CONV_DIM)], axis=1)
    return (out.reshape(b, t, d), xbc_all[:, -(CONV_W - 1):],
            h_last.reshape(b, N_HEADS, HEAD_DIM, SSD_STATE),
            k.reshape(b, t, N_HEADS, HEAD_DIM), v.reshape(b, t, N_HEADS, HEAD_DIM), logf3)


def kernel(x_prompt, x_sample, cache_conv, state_ssm, cache_fox_k, cache_fox_v, cache_fox_logf, g_mix, w_in, f_bias, g_q, g_k, w_conv, b_conv, dt_bias, a_log, d_skip, g_ssd, w_out, g_ffn, w_gate, w_up, w_down):
    depth = g_mix.shape[0]
    yp, ys = x_prompt, x_sample
    bp = x_prompt.shape[0]
    outs_p = [[] for _ in range(5)]
    outs_s = [[] for _ in range(5)]
    for i in range(depth):
        wts = (g_mix[i], _pack_w_in(w_in[i]), f_bias[i], g_q[i], g_k[i], w_conv[i], b_conv[i], dt_bias[i],
               a_log[i], d_skip[i], g_ssd[i], w_out[i].astype(BF16), g_ffn[i],
               w_gate[i].astype(BF16), w_up[i].astype(BF16), w_down[i].astype(BF16))
        yp, *rest = _layer(
            yp, jnp.zeros((bp, CONV_W - 1, CONV_DIM), F32),
            jnp.zeros((bp, N_HEADS, HEAD_DIM, SSD_STATE), F32), None, None, None, wts)
        for lst, val in zip(outs_p, rest):
            lst.append(val)
        ys, *rest = _layer(
            ys, cache_conv[i], state_ssm[i], cache_fox_k[i], cache_fox_v[i], cache_fox_logf[i], wts)
        for lst, val in zip(outs_s, rest):
            lst.append(val)
    return (yp, ys, *[jnp.stack(l) for l in outs_p], *[jnp.stack(l) for l in outs_s])
```

```python
import functools

import jax
import jax.numpy as jnp
import numpy as np
from jax import lax
from jax.experimental import pallas as pl
from jax.experimental.pallas import tpu as pltpu

F32 = jnp.float32
BF16 = jnp.bfloat16

EPS = 1e-6
HEAD_DIM = 64
N_HEADS = 16
FOX_DIM = 1024
SSD_DIM = 1024
SSD_GROUPS = 2
SSD_STATE = 128
GROUP_DIM = SSD_DIM // SSD_GROUPS
CONV_W = 4
CONV_DIM = SSD_DIM + 2 * SSD_GROUPS * SSD_STATE
LANES = 128
SUBLANES = 8
MXU_DIM = 256
NEG_BIG = -1e30
LOG2E = 1.4426950408889634
VMEM_LIMIT = 56 * 1024 * 1024
INPROJ_TOKENS = MXU_DIM
FFN_TOKENS = 2 * MXU_DIM
FFN_FF_TILE = 2 * MXU_DIM
FBIAS_TIME_TILE = 8 * MXU_DIM

COL_Q, COL_K, COL_V, COL_Z = 0, FOX_DIM, 2 * FOX_DIM, 3 * FOX_DIM
COL_XBC = COL_Z + SSD_DIM
COL_F = COL_XBC + CONV_DIM
COL_DT = COL_F + LANES
IN_COLS = COL_DT + LANES


def _dot(a, b):
    return jnp.dot(a, b, preferred_element_type=F32)


def _dot_nt(a, b):
    return lax.dot_general(a, b, (((1,), (1,)), ((), ())), preferred_element_type=F32)


def _dot_tn(a, b):
    return lax.dot_general(a, b, (((0,), (0,)), ((), ())), preferred_element_type=F32)


def _split3(x):
    x1 = x.astype(BF16)
    r = x - x1.astype(F32)
    x2 = r.astype(BF16)
    r = r - x2.astype(F32)
    return x1, x2, r.astype(BF16)


def _sel_dot(sel, x):
    x1, x2, x3 = _split3(x)
    return _dot(sel, x1) + _dot(sel, x2) + _dot(sel, x3)


def _dot_sel2(x, sel):
    x1 = x.astype(BF16)
    x2 = (x - x1.astype(F32)).astype(BF16)
    return _dot(x1, sel) + _dot(x2, sel)


def _sel_dot_nt(sel, x):
    x1, x2, x3 = _split3(x)
    return _dot_nt(sel, x1) + _dot_nt(sel, x2) + _dot_nt(sel, x3)


def _sigmoid(x):
    return 0.5 * jnp.tanh(0.5 * x) + 0.5


def _softplus(x):
    return jnp.maximum(x, 0.0) + jnp.log1p(jnp.exp(-jnp.abs(x)))


def _resident(shape):
    nd = len(shape)
    return pl.BlockSpec(shape, lambda *_: (0,) * nd, pipeline_mode=pl.Buffered(1))


def _params(sem):
    return pltpu.CompilerParams(dimension_semantics=sem, vmem_limit_bytes=VMEM_LIMIT)


def _inproj_kernel(x_ref, gmix_ref, w_ref, e_ref, gq_ref, gk_ref, fb_ref, dtb_ref,
                   q_ref, k_ref, kb_ref, v_ref, vb_ref, z_ref, xbc_ref, logf_ref, dt_ref, *, v_transposed):
    x = x_ref[...]
    ms = jnp.mean(x * x, axis=-1, keepdims=True)
    h = (x * lax.rsqrt(ms + EPS) * gmix_ref[...]).astype(BF16)
    e = e_ref[...]

    def head_rms(y, g):
        ysq = y * y
        hi = ysq.astype(BF16)
        lo = (ysq - hi.astype(F32)).astype(BF16)
        parts = []
        for c in range(FOX_DIM // MXU_DIM):
            sl = slice(MXU_DIM * c, MXU_DIM * (c + 1))
            parts.append(_dot(hi[:, sl], e) + _dot(lo[:, sl], e))
        ss = jnp.concatenate(parts, axis=-1)
        return y * lax.rsqrt(ss * (1.0 / HEAD_DIM) + EPS) * g

    q = _dot(h, w_ref[:, COL_Q:COL_Q + FOX_DIM])
    q_ref[...] = (head_rms(q, gq_ref[...]) * (HEAD_DIM ** -0.5 * LOG2E)).astype(BF16)
    k = head_rms(_dot(h, w_ref[:, COL_K:COL_K + FOX_DIM]), gk_ref[...])
    for hd in range(N_HEADS):
        k_ref[:, hd, :] = k[:, HEAD_DIM * hd:HEAD_DIM * (hd + 1)]
    kb_ref[...] = k.astype(BF16)
    v = _dot(h, w_ref[:, COL_V:COL_V + FOX_DIM])
    for hd in range(N_HEADS):
        v_ref[:, hd, :] = v[:, HEAD_DIM * hd:HEAD_DIM * (hd + 1)]
    if v_transposed:
        vb_ref[0] = v.T.astype(BF16)
    else:
        vb_ref[...] = v.astype(BF16)
    z_ref[...] = _dot(h, w_ref[:, COL_Z:COL_Z + SSD_DIM])
    xbc_ref[...] = _dot(h, w_ref[:, COL_XBC:COL_XBC + CONV_DIM])
    f_raw = _dot(h, w_ref[:, COL_F:COL_F + LANES])[:, :N_HEADS]
    logf_ref[...] = -_softplus(-(f_raw + fb_ref[...]))
    dt_raw = _dot(h, w_ref[:, COL_DT:COL_DT + LANES])[:, :N_HEADS]
    dt_ref[...] = _softplus(dt_raw + dtb_ref[...])


def _inproj(x2d, seq, g_mix, w_all, g_q, g_k, f_bias, dt_bias):
    n, d = x2d.shape
    tm = min(INPROJ_TOKENS, n)
    assert n % tm == 0 and n % seq == 0
    v_transposed = seq % tm == 0
    if v_transposed:
        tiles = seq // tm
        vb_shape = jax.ShapeDtypeStruct((n // seq, FOX_DIM, seq), BF16)
        vb_spec = pl.BlockSpec((1, FOX_DIM, tm), lambda i: (i // tiles, 0, i % tiles))
    else:
        vb_shape = jax.ShapeDtypeStruct((n, FOX_DIM), BF16)
        vb_spec = pl.BlockSpec((tm, FOX_DIM), lambda i: (i, 0))
    blk = jnp.arange(MXU_DIM) // HEAD_DIM
    e = (blk[:, None] == blk[None, :]).astype(BF16)
    row = lambda w: pl.BlockSpec((tm, w), lambda i: (i, 0))
    heads = pl.BlockSpec((tm, N_HEADS, HEAD_DIM), lambda i: (i, 0, 0))
    out_shapes = (
        jax.ShapeDtypeStruct((n, FOX_DIM), BF16),
        jax.ShapeDtypeStruct((n, N_HEADS, HEAD_DIM), F32),
        jax.ShapeDtypeStruct((n, FOX_DIM), BF16),
        jax.ShapeDtypeStruct((n, N_HEADS, HEAD_DIM), F32),
        vb_shape,
        jax.ShapeDtypeStruct((n, SSD_DIM), F32),
        jax.ShapeDtypeStruct((n, CONV_DIM), F32),
        jax.ShapeDtypeStruct((n, N_HEADS), F32),
        jax.ShapeDtypeStruct((n, N_HEADS), F32),
    )
    return pl.pallas_call(
        functools.partial(_inproj_kernel, v_transposed=v_transposed),
        grid=(n // tm,),
        in_specs=[row(d), _resident((1, d)), _resident((d, IN_COLS)), _resident((MXU_DIM, MXU_DIM)),
                  _resident((1, FOX_DIM)), _resident((1, FOX_DIM)),
                  _resident((1, N_HEADS)), _resident((1, N_HEADS))],
        out_specs=(row(FOX_DIM), heads, row(FOX_DIM), heads, vb_spec,
                   row(SSD_DIM), row(CONV_DIM), row(N_HEADS), row(N_HEADS)),
        out_shape=out_shapes,
        compiler_params=_params(("arbitrary",)),
        name="inproj",
    )(x2d, g_mix.reshape(1, d), w_all, e,
      jnp.tile(g_q, N_HEADS).reshape(1, FOX_DIM), jnp.tile(g_k, N_HEADS).reshape(1, FOX_DIM),
      f_bias.reshape(1, N_HEADS), dt_bias.reshape(1, N_HEADS))


ACC_ROWS = HEAD_DIM + 16
N_EXTRA = 3
Q_EXTRA_SPLIT = 2 * N_EXTRA
PAIR_EXTRA = 16


def _placement_constants():
    pq = np.zeros((N_EXTRA, N_HEADS, LANES), np.float32)
    pk = np.zeros((N_EXTRA, N_HEADS, LANES), np.float32)
    oq = np.zeros((1, LANES), np.float32)
    ok = np.zeros((1, LANES), np.float32)
    for h in range(N_HEADS):
        base = PAIR_EXTRA * (h // 2)
        for c in range(N_EXTRA):
            if h % 2 == 0:
                pq[c, h, base + c] = 1.0
                pk[c, h, base + N_EXTRA + c] = -1.0
                oq[0, base + N_EXTRA + c] = 1.0
                ok[0, base + c] = 1.0
            else:
                pq[c, h, base + 3 * N_EXTRA + c] = 1.0
                pk[c, h, base + 2 * N_EXTRA + c] = -1.0
                oq[0, base + 2 * N_EXTRA + c] = 1.0
                ok[0, base + 3 * N_EXTRA + c] = 1.0
    return (jnp.asarray(pq, BF16), jnp.asarray(pk, BF16), jnp.asarray(oq), jnp.asarray(ok))


def _fbias_kernel(logf_ref, pq_ref, pk_ref, oq_ref, ok_ref, kx_ref, qx_ref, carry_ref, *, blk, nblk):
    r = lax.broadcasted_iota(jnp.int32, (blk, blk), 0)
    c = lax.broadcasted_iota(jnp.int32, (blk, blk), 1)
    ltri = jnp.where(r >= c, 1.0, 0.0).astype(BF16)

    @pl.when(pl.program_id(1) == 0)
    def _():
        carry_ref[...] = jnp.zeros(carry_ref.shape, F32)

    lane = lax.broadcasted_iota(jnp.int32, (blk, N_EXTRA * N_HEADS), 1)

    def body(i, carry):
        off = pl.multiple_of(i * blk, blk)
        f = _sel_dot(ltri, logf_ref[0, pl.ds(off, blk), :]) + carry
        p1, p2, p3 = _split3(f * LOG2E)
        pieces = jnp.where(lane < N_HEADS, p1, jnp.where(lane < 2 * N_HEADS, p2, p3))
        kx_ref[0, pl.ds(off, blk), :] = (ok_ref[...] + _dot(pieces, pk_ref[...])).astype(BF16)
        qx_ref[0, pl.ds(off, blk), :] = (oq_ref[...] + _dot(pieces, pq_ref[...])).astype(BF16)
        return f[blk - 1:blk, :]

    carry_ref[...] = lax.fori_loop(0, nblk, body, carry_ref[...], unroll=True)


def _fbias(logf3, blk):
    b, t, _ = logf3.shape
    tt = t if t <= 2 * FBIAS_TIME_TILE else FBIAS_TIME_TILE
    assert t % tt == 0 and tt % blk == 0
    pq, pk, oq, ok = _placement_constants()
    width = N_EXTRA * N_HEADS
    out = jax.ShapeDtypeStruct((b, t, LANES), BF16)
    tile = lambda w: pl.BlockSpec((1, tt, w), lambda bi, ti: (bi, ti, 0))
    return pl.pallas_call(
        functools.partial(_fbias_kernel, blk=blk, nblk=tt // blk),
        grid=(b, t // tt),
        in_specs=[tile(width), _resident((width, LANES)), _resident((width, LANES)),
                  _resident((1, LANES)), _resident((1, LANES))],
        out_specs=(tile(LANES), tile(LANES)),
        out_shape=(out, out),
        scratch_shapes=[pltpu.VMEM((1, width), F32)],
        compiler_params=_params(("arbitrary", "arbitrary")),
        name="fbias",
    )(jnp.tile(logf3, (1, 1, N_EXTRA)), pq.reshape(width, LANES), pk.reshape(width, LANES), oq, ok)


def _pair_selectors():
    sel = np.zeros((N_HEADS // 2, LANES, LANES), np.float32)
    for p in range(N_HEADS // 2):
        for e in range(PAIR_EXTRA):
            sel[p, PAIR_EXTRA * p + e, e] = 1.0
    return jnp.asarray(sel, BF16)


def _attn_kernel(q_ref, qx_ref, k_ref, kx_ref, sel_ref, vt_ref, o_ref, kcat_ref, qs_ref, acc_ref, sa_ref, sb_ref,
                 *, tq, tsub, tk, past, tiles):
    i = pl.program_id(2)
    nsub = tq // tsub
    chains = [(sb, a) for sb in range(nsub) for a in range(2)]
    sel = sel_ref[0]

    @pl.when(i == 0)
    def _():
        kcat_ref[:, 0:LANES] = k_ref[0]

        def fill(r, carry):
            off = pl.multiple_of(r * tk, tk)
            kcat_ref[pl.ds(off, tk), LANES:2 * LANES] = _dot(kx_ref[0, pl.ds(off, tk), :], sel).astype(BF16)
            return carry

        lax.fori_loop(0, kcat_ref.shape[0] // tk, fill, 0)

    lane = lax.broadcasted_iota(jnp.int32, (tsub, LANES), 1)
    nct = len(chains)
    for h in range(tiles):
        for sb in range(nsub):
            rows = slice(tq * h + tsub * sb, tq * h + tsub * (sb + 1))
            q2 = q_ref[0, rows, :].astype(F32)
            qx2 = _dot(qx_ref[0, rows, :], sel)
            zero = jnp.zeros_like(q2)
            for a, keep_q, keep_x in ((0, lane < HEAD_DIM, lane < Q_EXTRA_SPLIT),
                                      (1, lane >= HEAD_DIM, lane >= Q_EXTRA_SPLIT)):
                c = nct * h + 2 * sb + a
                qs_ref[c, 0:LANES, :] = jnp.where(keep_q, q2, zero).T.astype(BF16)
                qs_ref[c, LANES:2 * LANES, :] = jnp.where(keep_x, qx2, zero).T.astype(BF16)
    acc_ref[...] = jnp.zeros(acc_ref.shape, F32)

    def qk_stage(j, s_ref, h, masked=False):
        off = pl.multiple_of(j * tk, tk)
        kc = kcat_ref[pl.ds(off, tk), :]
        q_lo = past + (i * tiles + h) * tq
        mx = []
        for c, (sb, a) in enumerate(chains):
            s = _dot(kc, qs_ref[nct * h + c])
            if masked:
                kpos = off + lax.broadcasted_iota(jnp.int32, (tk, tsub), 0)
                qpos = q_lo + tsub * sb + lax.broadcasted_iota(jnp.int32, (tk, tsub), 1)
                s = jnp.where(kpos <= qpos, s, NEG_BIG)
            s_ref[c] = s
            mx.append(jnp.max(s, axis=0, keepdims=True))
        return tuple(mx)

    def sm_stage(j, s_ref, mx, state, h):
        off = pl.multiple_of(j * tk, tk)
        ones = jnp.ones((ACC_ROWS - HEAD_DIM, tk), BF16)
        vts = [jnp.concatenate([vt_ref[0, HEAD_DIM * a:HEAD_DIM * (a + 1), pl.ds(off, tk)], ones], axis=0)
               for a in range(2)]
        out, probs = [], []
        for c, (sb, a) in enumerate(chains):
            m_new = jnp.maximum(state[c], mx[c])
            probs.append((jnp.exp2(state[c] - m_new), jnp.exp2(s_ref[c] - m_new).astype(BF16)))
            out.append(m_new)
        for c, (sb, a) in enumerate(chains):
            alpha, p = probs[c]
            acc_ref[nct * h + c] = alpha * acc_ref[nct * h + c] + _dot(vts[a], p)
        return tuple(out)

    def finalize(state, h):
        for sb in range(nsub):
            ot = jnp.concatenate([acc_ref[nct * h + 2 * sb + a, 0:HEAD_DIM, :]
                                  * (1.0 / acc_ref[nct * h + 2 * sb + a, HEAD_DIM:HEAD_DIM + 1, :])
                                  for a in range(2)], axis=0)
            rows = slice(tq * h + tsub * sb, tq * h + tsub * (sb + 1))
            o_ref[0, rows, :] = ot.T.astype(BF16)

    def pair_loop(n_pairs, first, h, carry):
        def body(u, carry):
            jprev, mx_a, state = carry
            mx_b = qk_stage(first + 2 * u, sb_ref, h)
            state = sm_stage(jprev, sa_ref, mx_a, state, h)
            mx_a = qk_stage(first + 2 * u + 1, sa_ref, h)
            state = sm_stage(first + 2 * u, sb_ref, mx_b, state, h)
            return first + 2 * u + 1, mx_a, state
        return lax.fori_loop(0, n_pairs, body, carry)

    init = tuple(jnp.full((1, tsub), NEG_BIG, F32) for _ in chains)
    if tiles == 1:
        jm = (past + i * tq) // tk
        mx0 = qk_stage(jm, sa_ref, 0, masked=True)
        carry = pair_loop(jm // 2, 0, 0, (jm, mx0, init))

        def odd_tail(args):
            jprev, mx_a, state = args
            mx_b = qk_stage(jm - 1, sb_ref, 0)
            state = sm_stage(jprev, sa_ref, mx_a, state, 0)
            return sm_stage(jm - 1, sb_ref, mx_b, state, 0)

        def even_tail(args):
            jprev, mx_a, state = args
            return sm_stage(jprev, sa_ref, mx_a, state, 0)

        finalize(lax.cond(jm % 2 == 1, odd_tail, even_tail, carry), 0)
    else:
        ja, jb = 2 * i, 2 * i + 1
        mx0 = qk_stage(ja, sa_ref, 0, masked=True)
        jprev, mx_a, st_a = pair_loop(i, 0, 0, (ja, mx0, init))
        mx_b = qk_stage(jb, sb_ref, 1, masked=True)
        st_a = sm_stage(jprev, sa_ref, mx_a, st_a, 0)
        mx_a = qk_stage(0, sa_ref, 1)
        st_b = sm_stage(jb, sb_ref, mx_b, init, 1)
        finalize(st_a, 0)
        jprev, mx_a, st_b = pair_loop(i, 1, 1, (0, mx_a, st_b))
        finalize(sm_stage(jprev, sa_ref, mx_a, st_b, 1), 1)


def _attention(q, qx, k, kx, vt, *, past, tq, tk):
    b, t, _ = q.shape
    tkeys = k.shape[1]
    npairs = N_HEADS // 2
    assert t % tq == 0 and tkeys % tk == 0 and past % tq == 0 and tk % tq == 0
    tsub = min(tq, MXU_DIM)
    assert tq % tsub == 0
    tiles = 2 if (past == 0 and tq == tk and t % (2 * tq) == 0) else 1
    rows = tiles * tq
    nchain = 2 * (tq // tsub)
    qoff = past // rows
    assert past % rows == 0
    nbuf = pl.Buffered(1 if t // rows > 1 else 2)
    return pl.pallas_call(
        functools.partial(_attn_kernel, tq=tq, tsub=tsub, tk=tk, past=past, tiles=tiles),
        grid=(b, npairs, t // rows),
        in_specs=[pl.BlockSpec((1, rows, LANES), lambda bi, p, i: (bi, i, p)),
                  pl.BlockSpec((1, rows, LANES), lambda bi, p, i: (bi, qoff + i, 0)),
                  pl.BlockSpec((1, tkeys, LANES), lambda bi, p, i: (bi, 0, p), pipeline_mode=nbuf),
                  pl.BlockSpec((1, tkeys, LANES), lambda bi, p, i: (bi, 0, 0), pipeline_mode=nbuf),
                  pl.BlockSpec((1, LANES, LANES), lambda bi, p, i: (p, 0, 0)),
                  pl.BlockSpec((1, LANES, tkeys), lambda bi, p, i: (bi, p, 0), pipeline_mode=nbuf)],
        out_specs=pl.BlockSpec((1, rows, LANES), lambda bi, p, i: (bi, i, p)),
        out_shape=jax.ShapeDtypeStruct((b, t, FOX_DIM), BF16),
        scratch_shapes=[pltpu.VMEM((tkeys, 2 * LANES), BF16),
                        pltpu.VMEM((tiles * nchain, 2 * LANES, tsub), BF16),
                        pltpu.VMEM((tiles * nchain, ACC_ROWS, tsub), F32),
                        pltpu.VMEM((nchain, tk, tsub), F32),
                        pltpu.VMEM((nchain, tk, tsub), F32)],
        compiler_params=_params(("arbitrary", "arbitrary", "arbitrary")),
        name="fox_attention",
    )(q, qx, k, kx, _pair_selectors(), vt)


def _ssd_kernel(xbc_ref, z_ref, dt_ref, cprev_ref, sprev_ref, wconv_ref, bconv_ref, alog_ref,
                dskip_ref, gssd_ref, rexp_ref,
                y_ref, hfin_ref, ext_ref, state_ref, *, L):
    c = pl.program_id(1)

    @pl.when(c == 0)
    def _():
        state_ref[...] = sprev_ref[0]
        ext_ref[...] = cprev_ref[0]

    x_tile = xbc_ref[0]
    ext = jnp.concatenate([ext_ref[...], x_tile], axis=0)
    w = wconv_ref[...]
    pre = bconv_ref[...] + x_tile * w[CONV_W - 1:CONV_W, :]
    for back in range(1, CONV_W):
        shifted = pltpu.roll(ext, back, axis=0)[SUBLANES:, :]
        pre = pre + shifted * w[CONV_W - 1 - back:CONV_W - back, :]
    ext_ref[...] = x_tile[L - SUBLANES:L, :]
    u = pre * _sigmoid(pre)
    xs = u[:, 0:SSD_DIM]
    bmat = u[:, SSD_DIM:SSD_DIM + SSD_GROUPS * SSD_STATE].astype(BF16)
    cmat = u[:, SSD_DIM + SSD_GROUPS * SSD_STATE:].astype(BF16)

    dt = dt_ref[0]
    a = dt * (-jnp.exp(alog_ref[...]))
    r = lax.broadcasted_iota(jnp.int32, (L, L), 0)
    cc = lax.broadcasted_iota(jnp.int32, (L, L), 1)
    causal = r >= cc
    ltri = jnp.where(causal, 1.0, 0.0).astype(BF16)
    cum = _sel_dot(ltri, a)
    e_r = lax.broadcasted_iota(jnp.int32, (N_HEADS, N_HEADS), 0)
    e_c = lax.broadcasted_iota(jnp.int32, (N_HEADS, N_HEADS), 1)
    eye = jnp.where(e_r == e_c, 1.0, 0.0).astype(BF16)
    cum_t = _sel_dot_nt(eye, cum)
    last = cum[L - 1:L, :]

    dt_t = _sel_dot_nt(eye, dt)

    rexp = rexp_ref[...]
    wendx = _dot_sel2(dt * jnp.exp(last - cum), rexp)
    ecumx = _dot_sel2(jnp.exp(cum), rexp)
    xs_b = xs.astype(BF16)
    xw_b = (xs * wendx).astype(BF16)
    elast_b = jnp.broadcast_to(jnp.exp(cum_t[:, L - 1:L]), (N_HEADS, SSD_STATE))
    lane = lax.broadcasted_iota(jnp.int32, (L, LANES), 1)

    heads_per_group = N_HEADS // SSD_GROUPS
    ys = []
    for g in range(SSD_GROUPS):
        gs = slice(GROUP_DIM * g, GROUP_DIM * (g + 1))
        cg = cmat[:, SSD_STATE * g:SSD_STATE * (g + 1)]
        bg = bmat[:, SSD_STATE * g:SSD_STATE * (g + 1)]
        cb = _dot_nt(cg, bg)
        st = state_ref[gs, :]
        yoff = _dot_nt(cg, st.astype(BF16))
        for pr in range(heads_per_group // 2):
            ps = slice(GROUP_DIM * g + LANES * pr, GROUP_DIM * g + LANES * (pr + 1))
            slab = xs_b[:, ps]
            res = []
            for hh in range(2):
                hd = heads_per_group * g + 2 * pr + hh
                seg = cum[:, hd:hd + 1] - cum_t[hd:hd + 1, :]
                dec = jnp.exp(jnp.where(causal, seg, -jnp.inf))
                res.append(_dot((cb * dec * dt_t[hd:hd + 1, :]).astype(BF16), slab))
            ydiag = jnp.where(lane < HEAD_DIM, res[0], res[1])
            ys.append(ydiag + yoff[:, LANES * pr:LANES * (pr + 1)] * ecumx[:, ps])
        snew = _dot_tn(xw_b[:, gs], bg)
        for hh in range(heads_per_group):
            hd = heads_per_group * g + hh
            rows = slice(HEAD_DIM * hd, HEAD_DIM * (hd + 1))
            state_ref[rows, :] = (elast_b[hd:hd + 1, :] * st[HEAD_DIM * hh:HEAD_DIM * (hh + 1), :]
                                  + snew[HEAD_DIM * hh:HEAD_DIM * (hh + 1), :])

    y = jnp.concatenate(ys, axis=-1) + dskip_ref[...] * xs
    zt = z_ref[0]
    ug = y * (zt * _sigmoid(zt))
    outs = []
    for g in range(SSD_GROUPS):
        ugg = ug[:, GROUP_DIM * g:GROUP_DIM * (g + 1)]
        outs.append(ugg * lax.rsqrt(jnp.mean(ugg * ugg, axis=-1, keepdims=True) + EPS))
    y_ref[0] = (jnp.concatenate(outs, axis=-1) * gssd_ref[...]).astype(BF16)

    @pl.when(c == pl.num_programs(1) - 1)
    def _():
        hfin_ref[0] = state_ref[...]


def _ssd(xbc, z, dt, conv_prev8, ssm_prev, w_conv, b_conv, a_log, d_skip, g_ssd, *, L):
    b, t, _ = xbc.shape
    assert t % L == 0 and L % SUBLANES == 0
    head_of_lane = jnp.arange(SSD_DIM) // HEAD_DIM
    rexp = (jnp.arange(N_HEADS)[:, None] == head_of_lane[None, :]).astype(BF16)
    tile = lambda w: pl.BlockSpec((1, L, w), lambda bi, ci: (bi, ci, 0))
    per_b = lambda s: pl.BlockSpec((1,) + s, lambda bi, ci: (bi, 0, 0))
    return pl.pallas_call(
        functools.partial(_ssd_kernel, L=L),
        grid=(b, t // L),
        in_specs=[tile(CONV_DIM), tile(SSD_DIM), tile(N_HEADS),
                  per_b((SUBLANES, CONV_DIM)), per_b((SSD_DIM, SSD_STATE)),
                  _resident((CONV_W, CONV_DIM)), _resident((1, CONV_DIM)), _resident((1, N_HEADS)),
                  _resident((1, SSD_DIM)), _resident((1, SSD_DIM)),
                  _resident((N_HEADS, SSD_DIM))],
        out_specs=(tile(SSD_DIM), per_b((SSD_DIM, SSD_STATE))),
        out_shape=(jax.ShapeDtypeStruct((b, t, SSD_DIM), BF16),
                   jax.ShapeDtypeStruct((b, SSD_DIM, SSD_STATE), F32)),
        scratch_shapes=[pltpu.VMEM((SUBLANES, CONV_DIM), F32),
                        pltpu.VMEM((SSD_DIM, SSD_STATE), F32)],
        compiler_params=_params(("arbitrary", "arbitrary")),
        name="ssd_scan",
    )(xbc, z, dt, conv_prev8, ssm_prev, w_conv, b_conv.reshape(1, CONV_DIM), a_log.reshape(1, N_HEADS),
      jnp.repeat(d_skip, HEAD_DIM).reshape(1, SSD_DIM), g_ssd.reshape(1, SSD_DIM), rexp)


def _ffn_kernel(x_ref, attn_ref, y_ref, wo_ref, g_ref, wg_ref, wu_ref, wd_ref, o_ref, hf_ref):
    j = pl.program_id(1)

    @pl.when(j == 0)
    def _():
        mix = _dot(attn_ref[...], wo_ref[0:FOX_DIM, :]) + _dot(y_ref[...], wo_ref[FOX_DIM:, :])
        x = x_ref[...] + mix
        ms = jnp.mean(x * x, axis=-1, keepdims=True)
        hf_ref[...] = (x * lax.rsqrt(ms + EPS) * g_ref[...]).astype(BF16)
        o_ref[...] = x

    hf = hf_ref[...]
    gate = _dot(hf, wg_ref[...])
    act = (gate * _sigmoid(gate) * _dot(hf, wu_ref[...])).astype(BF16)
    o_ref[...] += _dot(act, wd_ref[...])


def _outproj_ffn(x2d, attn, y, w_out, g_ffn, w_gate, w_up, w_down):
    n, d = x2d.shape
    dff = w_gate.shape[1]
    tm, tf = min(FFN_TOKENS, n), FFN_FF_TILE
    assert n % tm == 0 and dff % tf == 0
    row = lambda w: pl.BlockSpec((tm, w), lambda i, j: (i, 0))
    return pl.pallas_call(
        _ffn_kernel,
        grid=(n // tm, dff // tf),
        in_specs=[row(d), row(FOX_DIM), row(SSD_DIM), _resident((FOX_DIM + SSD_DIM, d)),
                  _resident((1, d)),
                  pl.BlockSpec((d, tf), lambda i, j: (0, j)),
                  pl.BlockSpec((d, tf), lambda i, j: (0, j)),
                  pl.BlockSpec((tf, d), lambda i, j: (j, 0))],
        out_specs=pl.BlockSpec((tm, d), lambda i, j: (i, 0)),
        out_shape=jax.ShapeDtypeStruct((n, d), F32),
        scratch_shapes=[pltpu.VMEM((tm, d), BF16)],
        compiler_params=_params(("arbitrary", "arbitrary")),
        name="ffn",
    )(x2d, attn, y, w_out, g_ffn.reshape(1, d), w_gate, w_up, w_down)


def _pack_w_in(w_in):
    d = w_in.shape[0]
    o_f = 3 * FOX_DIM
    o_z = o_f + N_HEADS
    o_xbc = o_z + SSD_DIM
    o_dt = o_xbc + CONV_DIM
    pad = jnp.zeros((d, LANES - N_HEADS), w_in.dtype)
    return jnp.concatenate([w_in[:, :o_f], w_in[:, o_z:o_xbc], w_in[:, o_xbc:o_dt],
                            w_in[:, o_f:o_z], pad, w_in[:, o_dt:], pad], axis=1).astype(BF16)


def _tile_plan(t, past):
    if past == 0:
        tile = min(4 * MXU_DIM, t)
        return dict(attn_tq=tile, attn_tk=tile, fbias_blk=min(MXU_DIM, t), ssd_chunk=min(LANES, t))
    return dict(attn_tq=t, attn_tk=LANES, fbias_blk=LANES, ssd_chunk=t)


def _layer(x, conv_prev, ssm_prev, k_prev, v_prev, logf_prev, wts):
    (g_mix, w_all, f_bias, g_q, g_k, w_conv, b_conv, dt_bias, a_log, d_skip, g_ssd,
     w_out, g_ffn, w_gate, w_up, w_down) = wts
    b, t, d = x.shape
    n = b * t
    x2d = x.reshape(n, d)
    qb, k, kb, v, vb, z, xbc, logf, dt = _inproj(x2d, t, g_mix, w_all, g_q, g_k, f_bias, dt_bias)

    past = 0 if k_prev is None else k_prev.shape[1]
    plan = _tile_plan(t, past)
    kb3, logf3 = kb.reshape(b, t, FOX_DIM), logf.reshape(b, t, N_HEADS)
    vt3 = vb if vb.ndim == 3 else jnp.transpose(vb.reshape(b, t, FOX_DIM), (0, 2, 1))
    logf_all = logf3
    if past:
        kb3 = jnp.concatenate([k_prev.reshape(b, past, FOX_DIM).astype(BF16), kb3], axis=1)
        vt_prev = jnp.transpose(v_prev.reshape(b, past, FOX_DIM).astype(BF16), (0, 2, 1))
        vt3 = jnp.concatenate([vt_prev, vt3], axis=2)
        logf_all = jnp.concatenate([logf_prev.astype(F32), logf3], axis=1)
    tkeys = past + t
    tk = plan["attn_tk"]
    if tkeys % tk:
        tk = -(-tkeys // LANES) * LANES
        grow = ((0, 0), (0, tk - tkeys), (0, 0))
        kb3, logf_all = jnp.pad(kb3, grow), jnp.pad(logf_all, grow)
        vt3 = jnp.pad(vt3, ((0, 0), (0, 0), (0, tk - tkeys)))
    kx, qx = _fbias(logf_all, min(plan["fbias_blk"], tk))
    attn = _attention(qb.reshape(b, t, FOX_DIM), qx, kb3, kx, vt3, past=past, tq=plan["attn_tq"], tk=tk)

    conv_prev8 = jnp.pad(conv_prev.astype(F32), ((0, 0), (SUBLANES - (CONV_W - 1), 0), (0, 0)))
    y, h_last = _ssd(xbc.reshape(b, t, CONV_DIM), z.reshape(b, t, SSD_DIM), dt.reshape(b, t, N_HEADS),
                     conv_prev8, ssm_prev.astype(F32).reshape(b, SSD_DIM, SSD_STATE),
                     w_conv, b_conv, a_log, d_skip, g_ssd, L=plan["ssd_chunk"])

    out = _outproj_ffn(x2d, attn.reshape(n, FOX_DIM), y.reshape(n, SSD_DIM), w_out, g_ffn, w_gate, w_up, w_down)

    xbc_all = jnp.concatenate([conv_prev.astype(F32), xbc.reshape(b, t, CONV_DIM)], axis=1)
    return (out.reshape(b, t, d), xbc_all[:, -(CONV_W - 1):],
            h_last.reshape(b, N_HEADS, HEAD_DIM, SSD_STATE),
            k.reshape(b, t, N_HEADS, HEAD_DIM), v.reshape(b, t, N_HEADS, HEAD_DIM), logf3)


def kernel(x_prompt, x_sample, cache_conv, state_ssm, cache_fox_k, cache_fox_v, cache_fox_logf, g_mix, w_in, f_bias, g_q, g_k, w_conv, b_conv, dt_bias, a_log, d_skip, g_ssd, w_out, g_ffn, w_gate, w_up, w_down):
    depth = g_mix.shape[0]
    yp, ys = x_prompt, x_sample
    bp = x_prompt.shape[0]
    outs_p = [[] for _ in range(5)]
    outs_s = [[] for _ in range(5)]
    for i in range(depth):
        wts = (g_mix[i], _pack_w_in(w_in[i]), f_bias[i], g_q[i], g_k[i], w_conv[i], b_conv[i], dt_bias[i],
               a_log[i], d_skip[i], g_ssd[i], w_out[i].astype(BF16), g_ffn[i],
               w_gate[i].astype(BF16), w_up[i].astype(BF16), w_down[i].astype(BF16))
        yp, *rest = _layer(
            yp, jnp.zeros((bp, CONV_W - 1, CONV_DIM), F32),
            jnp.zeros((bp, N_HEADS, HEAD_DIM, SSD_STATE), F32), None, None, None, wts)
        for lst, val in zip(outs_p, rest):
            lst.append(val)
        ys, *rest = _layer(
            ys, cache_conv[i], state_ssm[i], cache_fox_k[i], cache_fox_v[i], cache_fox_logf[i], wts)
        for lst, val in zip(outs_s, rest):
            lst.append(val)
    return (yp, ys, *[jnp.stack(l) for l in outs_p], *[jnp.stack(l) for l in outs_s])
```

```python
import functools

import jax
import jax.numpy as jnp
import numpy as np
from jax import lax
from jax.experimental import pallas as pl
from jax.experimental.pallas import tpu as pltpu

F32 = jnp.float32
BF16 = jnp.bfloat16

EPS = 1e-6
HEAD_DIM = 64
N_HEADS = 16
FOX_DIM = 1024
SSD_DIM = 1024
SSD_GROUPS = 2
SSD_STATE = 128
GROUP_DIM = SSD_DIM // SSD_GROUPS
CONV_W = 4
CONV_DIM = SSD_DIM + 2 * SSD_GROUPS * SSD_STATE
LANES = 128
SUBLANES = 8
MXU_DIM = 256
NEG_BIG = -1e30
LOG2E = 1.4426950408889634
VMEM_LIMIT = 56 * 1024 * 1024
INPROJ_TOKENS = MXU_DIM
FFN_TOKENS = 2 * MXU_DIM
FFN_FF_TILE = 2 * MXU_DIM
FBIAS_TIME_TILE = 8 * MXU_DIM

COL_Q, COL_K, COL_V, COL_Z = 0, FOX_DIM, 2 * FOX_DIM, 3 * FOX_DIM
COL_XBC = COL_Z + SSD_DIM
COL_F = COL_XBC + CONV_DIM
COL_DT = COL_F + LANES
IN_COLS = COL_DT + LANES


def _dot(a, b):
    return jnp.dot(a, b, preferred_element_type=F32)


def _dot_nt(a, b):
    return lax.dot_general(a, b, (((1,), (1,)), ((), ())), preferred_element_type=F32)


def _dot_tn(a, b):
    return lax.dot_general(a, b, (((0,), (0,)), ((), ())), preferred_element_type=F32)


def _split3(x):
    x1 = x.astype(BF16)
    r = x - x1.astype(F32)
    x2 = r.astype(BF16)
    r = r - x2.astype(F32)
    return x1, x2, r.astype(BF16)


def _sel_dot(sel, x):
    x1, x2, x3 = _split3(x)
    return _dot(sel, x1) + _dot(sel, x2) + _dot(sel, x3)


def _dot_sel2(x, sel):
    x1 = x.astype(BF16)
    x2 = (x - x1.astype(F32)).astype(BF16)
    return _dot(x1, sel) + _dot(x2, sel)


def _sel_dot_nt(sel, x):
    x1, x2, x3 = _split3(x)
    return _dot_nt(sel, x1) + _dot_nt(sel, x2) + _dot_nt(sel, x3)


def _sigmoid(x):
    return 0.5 * jnp.tanh(0.5 * x) + 0.5


def _softplus(x):
    return jnp.maximum(x, 0.0) + jnp.log1p(jnp.exp(-jnp.abs(x)))


def _resident(shape):
    nd = len(shape)
    return pl.BlockSpec(shape, lambda *_: (0,) * nd, pipeline_mode=pl.Buffered(1))


def _params(sem):
    return pltpu.CompilerParams(dimension_semantics=sem, vmem_limit_bytes=VMEM_LIMIT)


def _inproj_kernel(x_ref, gmix_ref, w_ref, e_ref, gq_ref, gk_ref, fb_ref, dtb_ref,
                   q_ref, k_ref, kb_ref, v_ref, vb_ref, z_ref, xbc_ref, logf_ref, dt_ref, *, v_transposed):
    x = x_ref[...]
    ms = jnp.mean(x * x, axis=-1, keepdims=True)
    h = (x * lax.rsqrt(ms + EPS) * gmix_ref[...]).astype(BF16)
    e = e_ref[...]

    def head_rms(y, g):
        ysq = (y * y).astype(BF16)
        ss = jnp.concatenate([_dot(ysq[:, MXU_DIM * c:MXU_DIM * (c + 1)], e) for c in range(FOX_DIM // MXU_DIM)],
                             axis=-1)
        return y * lax.rsqrt(ss * (1.0 / HEAD_DIM) + EPS) * g

    q = _dot(h, w_ref[:, COL_Q:COL_Q + FOX_DIM])
    q_ref[...] = (head_rms(q, gq_ref[...]) * (HEAD_DIM ** -0.5 * LOG2E)).astype(BF16)
    k = head_rms(_dot(h, w_ref[:, COL_K:COL_K + FOX_DIM]), gk_ref[...])
    for hd in range(N_HEADS):
        k_ref[:, hd, :] = k[:, HEAD_DIM * hd:HEAD_DIM * (hd + 1)]
    kb_ref[...] = k.astype(BF16)
    v = _dot(h, w_ref[:, COL_V:COL_V + FOX_DIM])
    for hd in range(N_HEADS):
        v_ref[:, hd, :] = v[:, HEAD_DIM * hd:HEAD_DIM * (hd + 1)]
    if v_transposed:
        vb_ref[0] = v.T.astype(BF16)
    else:
        vb_ref[...] = v.astype(BF16)
    z_ref[...] = _dot(h, w_ref[:, COL_Z:COL_Z + SSD_DIM])
    xbc_ref[...] = _dot(h, w_ref[:, COL_XBC:COL_XBC + CONV_DIM])
    f_raw = _dot(h, w_ref[:, COL_F:COL_F + LANES])[:, :N_HEADS]
    logf_ref[...] = -_softplus(-(f_raw + fb_ref[...]))
    dt_raw = _dot(h, w_ref[:, COL_DT:COL_DT + LANES])[:, :N_HEADS]
    dt_ref[...] = _softplus(dt_raw + dtb_ref[...])


def _inproj(x2d, seq, g_mix, w_all, g_q, g_k, f_bias, dt_bias):
    n, d = x2d.shape
    tm = min(INPROJ_TOKENS, n)
    assert n % tm == 0 and n % seq == 0
    v_transposed = seq % tm == 0
    if v_transposed:
        tiles = seq // tm
        vb_shape = jax.ShapeDtypeStruct((n // seq, FOX_DIM, seq), BF16)
        vb_spec = pl.BlockSpec((1, FOX_DIM, tm), lambda i: (i // tiles, 0, i % tiles))
    else:
        vb_shape = jax.ShapeDtypeStruct((n, FOX_DIM), BF16)
        vb_spec = pl.BlockSpec((tm, FOX_DIM), lambda i: (i, 0))
    blk = jnp.arange(MXU_DIM) // HEAD_DIM
    e = (blk[:, None] == blk[None, :]).astype(BF16)
    row = lambda w: pl.BlockSpec((tm, w), lambda i: (i, 0))
    heads = pl.BlockSpec((tm, N_HEADS, HEAD_DIM), lambda i: (i, 0, 0))
    out_shapes = (
        jax.ShapeDtypeStruct((n, FOX_DIM), BF16),
        jax.ShapeDtypeStruct((n, N_HEADS, HEAD_DIM), F32),
        jax.ShapeDtypeStruct((n, FOX_DIM), BF16),
        jax.ShapeDtypeStruct((n, N_HEADS, HEAD_DIM), F32),
        vb_shape,
        jax.ShapeDtypeStruct((n, SSD_DIM), F32),
        jax.ShapeDtypeStruct((n, CONV_DIM), F32),
        jax.ShapeDtypeStruct((n, N_HEADS), F32),
        jax.ShapeDtypeStruct((n, N_HEADS), F32),
    )
    return pl.pallas_call(
        functools.partial(_inproj_kernel, v_transposed=v_transposed),
        grid=(n // tm,),
        in_specs=[row(d), _resident((1, d)), _resident((d, IN_COLS)), _resident((MXU_DIM, MXU_DIM)),
                  _resident((1, FOX_DIM)), _resident((1, FOX_DIM)),
                  _resident((1, N_HEADS)), _resident((1, N_HEADS))],
        out_specs=(row(FOX_DIM), heads, row(FOX_DIM), heads, vb_spec,
                   row(SSD_DIM), row(CONV_DIM), row(N_HEADS), row(N_HEADS)),
        out_shape=out_shapes,
        compiler_params=_params(("arbitrary",)),
        name="inproj",
    )(x2d, g_mix.reshape(1, d), w_all, e,
      jnp.tile(g_q, N_HEADS).reshape(1, FOX_DIM), jnp.tile(g_k, N_HEADS).reshape(1, FOX_DIM),
      f_bias.reshape(1, N_HEADS), dt_bias.reshape(1, N_HEADS))


N_EXTRA = 3
Q_EXTRA_SPLIT = 2 * N_EXTRA
PAIR_EXTRA = 16


def _placement_constants():
    pq = np.zeros((N_EXTRA, N_HEADS, LANES), np.float32)
    pk = np.zeros((N_EXTRA, N_HEADS, LANES), np.float32)
    oq = np.zeros((1, LANES), np.float32)
    ok = np.zeros((1, LANES), np.float32)
    for h in range(N_HEADS):
        base = PAIR_EXTRA * (h // 2)
        for c in range(N_EXTRA):
            if h % 2 == 0:
                pq[c, h, base + c] = 1.0
                pk[c, h, base + N_EXTRA + c] = -1.0
                oq[0, base + N_EXTRA + c] = 1.0
                ok[0, base + c] = 1.0
            else:
                pq[c, h, base + 3 * N_EXTRA + c] = 1.0
                pk[c, h, base + 2 * N_EXTRA + c] = -1.0
                oq[0, base + 2 * N_EXTRA + c] = 1.0
                ok[0, base + 3 * N_EXTRA + c] = 1.0
    return (jnp.asarray(pq, BF16), jnp.asarray(pk, BF16), jnp.asarray(oq), jnp.asarray(ok))


def _fbias_kernel(logf_ref, pq_ref, pk_ref, oq_ref, ok_ref, kx_ref, qx_ref, carry_ref, *, blk, nblk):
    r = lax.broadcasted_iota(jnp.int32, (blk, blk), 0)
    c = lax.broadcasted_iota(jnp.int32, (blk, blk), 1)
    ltri = jnp.where(r >= c, 1.0, 0.0).astype(BF16)

    @pl.when(pl.program_id(1) == 0)
    def _():
        carry_ref[...] = jnp.zeros(carry_ref.shape, F32)

    lane = lax.broadcasted_iota(jnp.int32, (blk, N_EXTRA * N_HEADS), 1)

    def body(i, carry):
        off = pl.multiple_of(i * blk, blk)
        f = _sel_dot(ltri, logf_ref[0, pl.ds(off, blk), :]) + carry
        p1, p2, p3 = _split3(f * LOG2E)
        pieces = jnp.where(lane < N_HEADS, p1, jnp.where(lane < 2 * N_HEADS, p2, p3))
        kx_ref[0, pl.ds(off, blk), :] = (ok_ref[...] + _dot(pieces, pk_ref[...])).astype(BF16)
        qx_ref[0, pl.ds(off, blk), :] = (oq_ref[...] + _dot(pieces, pq_ref[...])).astype(BF16)
        return f[blk - 1:blk, :]

    carry_ref[...] = lax.fori_loop(0, nblk, body, carry_ref[...], unroll=True)


def _fbias(logf3, blk):
    b, t, _ = logf3.shape
    tt = t if t <= 2 * FBIAS_TIME_TILE else FBIAS_TIME_TILE
    assert t % tt == 0 and tt % blk == 0
    pq, pk, oq, ok = _placement_constants()
    width = N_EXTRA * N_HEADS
    out = jax.ShapeDtypeStruct((b, t, LANES), BF16)
    tile = lambda w: pl.BlockSpec((1, tt, w), lambda bi, ti: (bi, ti, 0))
    return pl.pallas_call(
        functools.partial(_fbias_kernel, blk=blk, nblk=tt // blk),
        grid=(b, t // tt),
        in_specs=[tile(width), _resident((width, LANES)), _resident((width, LANES)),
                  _resident((1, LANES)), _resident((1, LANES))],
        out_specs=(tile(LANES), tile(LANES)),
        out_shape=(out, out),
        scratch_shapes=[pltpu.VMEM((1, width), F32)],
        compiler_params=_params(("arbitrary", "arbitrary")),
        name="fbias",
    )(jnp.tile(logf3, (1, 1, N_EXTRA)), pq.reshape(width, LANES), pk.reshape(width, LANES), oq, ok)


def _pair_selectors():
    sel = np.zeros((N_HEADS // 2, LANES, LANES), np.float32)
    for p in range(N_HEADS // 2):
        for e in range(PAIR_EXTRA):
            sel[p, PAIR_EXTRA * p + e, e] = 1.0
    return jnp.asarray(sel, BF16)


def _attn_kernel(q_ref, qx_ref, k_ref, kx_ref, sel_ref, vt_ref, o_ref, kcat_ref, qs_ref, acc_ref, sa_ref, sb_ref,
                 *, tq, tsub, tk, past, tiles):
    i = pl.program_id(2)
    nsub = tq // tsub
    chains = [(sb, a) for sb in range(nsub) for a in range(2)]
    sel = sel_ref[0]

    @pl.when(i == 0)
    def _():
        kcat_ref[:, 0:LANES] = k_ref[0]

        def fill(r, carry):
            off = pl.multiple_of(r * tk, tk)
            kcat_ref[pl.ds(off, tk), LANES:2 * LANES] = _dot(kx_ref[0, pl.ds(off, tk), :], sel).astype(BF16)
            return carry

        lax.fori_loop(0, kcat_ref.shape[0] // tk, fill, 0)

    lane = lax.broadcasted_iota(jnp.int32, (tsub, LANES), 1)
    nct = len(chains)
    for h in range(tiles):
        for sb in range(nsub):
            rows = slice(tq * h + tsub * sb, tq * h + tsub * (sb + 1))
            q2 = q_ref[0, rows, :].astype(F32)
            qx2 = _dot(qx_ref[0, rows, :], sel)
            zero = jnp.zeros_like(q2)
            for a, keep_q, keep_x in ((0, lane < HEAD_DIM, lane < Q_EXTRA_SPLIT),
                                      (1, lane >= HEAD_DIM, lane >= Q_EXTRA_SPLIT)):
                c = nct * h + 2 * sb + a
                qs_ref[c, 0:LANES, :] = jnp.where(keep_q, q2, zero).T.astype(BF16)
                qs_ref[c, LANES:2 * LANES, :] = jnp.where(keep_x, qx2, zero).T.astype(BF16)
    acc_ref[...] = jnp.zeros(acc_ref.shape, F32)

    def qk_stage(j, s_ref, h, masked=False):
        off = pl.multiple_of(j * tk, tk)
        kc = kcat_ref[pl.ds(off, tk), :]
        q_lo = past + (i * tiles + h) * tq
        mx = []
        for c, (sb, a) in enumerate(chains):
            s = _dot(kc, qs_ref[nct * h + c])
            if masked:
                kpos = off + lax.broadcasted_iota(jnp.int32, (tk, tsub), 0)
                qpos = q_lo + tsub * sb + lax.broadcasted_iota(jnp.int32, (tk, tsub), 1)
                s = jnp.where(kpos <= qpos, s, NEG_BIG)
            s_ref[c] = s
            mx.append(jnp.max(s, axis=0, keepdims=True))
        return tuple(mx)

    def sm_stage(j, s_ref, mx, state, h):
        off = pl.multiple_of(j * tk, tk)
        vts = [vt_ref[0, HEAD_DIM * a:HEAD_DIM * (a + 1), pl.ds(off, tk)] for a in range(2)]
        out, probs = [], []
        for c, (sb, a) in enumerate(chains):
            m_prev, l_prev = state[c]
            m_new = jnp.maximum(m_prev, mx[c])
            alpha = jnp.exp2(m_prev - m_new)
            p = jnp.exp2(s_ref[c] - m_new)
            out.append((m_new, alpha * l_prev + jnp.sum(p, axis=0, keepdims=True)))
            probs.append((alpha, p.astype(BF16)))
        for c, (sb, a) in enumerate(chains):
            alpha, p = probs[c]
            acc_ref[nct * h + c] = alpha * acc_ref[nct * h + c] + _dot(vts[a], p)
        return tuple(out)

    def finalize(state, h):
        for sb in range(nsub):
            ot = jnp.concatenate([acc_ref[nct * h + 2 * sb + a] * (1.0 / state[2 * sb + a][1]) for a in range(2)],
                                 axis=0)
            rows = slice(tq * h + tsub * sb, tq * h + tsub * (sb + 1))
            o_ref[0, rows, :] = ot.T.astype(BF16)

    def pair_loop(n_pairs, first, h, carry):
        def body(u, carry):
            jprev, mx_a, state = carry
            mx_b = qk_stage(first + 2 * u, sb_ref, h)
            state = sm_stage(jprev, sa_ref, mx_a, state, h)
            mx_a = qk_stage(first + 2 * u + 1, sa_ref, h)
            state = sm_stage(first + 2 * u, sb_ref, mx_b, state, h)
            return first + 2 * u + 1, mx_a, state
        return lax.fori_loop(0, n_pairs, body, carry)

    init = tuple((jnp.full((1, tsub), NEG_BIG, F32), jnp.zeros((1, tsub), F32)) for _ in chains)
    if tiles == 1:
        jm = (past + i * tq) // tk
        mx0 = qk_stage(jm, sa_ref, 0, masked=True)
        carry = pair_loop(jm // 2, 0, 0, (jm, mx0, init))

        def odd_tail(args):
            jprev, mx_a, state = args
            mx_b = qk_stage(jm - 1, sb_ref, 0)
            state = sm_stage(jprev, sa_ref, mx_a, state, 0)
            return sm_stage(jm - 1, sb_ref, mx_b, state, 0)

        def even_tail(args):
            jprev, mx_a, state = args
            return sm_stage(jprev, sa_ref, mx_a, state, 0)

        finalize(lax.cond(jm % 2 == 1, odd_tail, even_tail, carry), 0)
    else:
        ja, jb = 2 * i, 2 * i + 1
        mx0 = qk_stage(ja, sa_ref, 0, masked=True)
        jprev, mx_a, st_a = pair_loop(i, 0, 0, (ja, mx0, init))
        mx_b = qk_stage(jb, sb_ref, 1, masked=True)
        st_a = sm_stage(jprev, sa_ref, mx_a, st_a, 0)
        mx_a = qk_stage(0, sa_ref, 1)
        st_b = sm_stage(jb, sb_ref, mx_b, init, 1)
        finalize(st_a, 0)
        jprev, mx_a, st_b = pair_loop(i, 1, 1, (0, mx_a, st_b))
        finalize(sm_stage(jprev, sa_ref, mx_a, st_b, 1), 1)


def _attention(q, qx, k, kx, vt, *, past, tq, tk):
    b, t, _ = q.shape
    tkeys = k.shape[1]
    npairs = N_HEADS // 2
    assert t % tq == 0 and tkeys % tk == 0 and past % tq == 0 and tk % tq == 0
    tsub = min(tq, MXU_DIM)
    assert tq % tsub == 0
    tiles = 2 if (past == 0 and tq == tk and t % (2 * tq) == 0) else 1
    rows = tiles * tq
    nchain = 2 * (tq // tsub)
    qoff = past // rows
    assert past % rows == 0
    nbuf = pl.Buffered(1 if t // rows > 1 else 2)
    return pl.pallas_call(
        functools.partial(_attn_kernel, tq=tq, tsub=tsub, tk=tk, past=past, tiles=tiles),
        grid=(b, npairs, t // rows),
        in_specs=[pl.BlockSpec((1, rows, LANES), lambda bi, p, i: (bi, i, p)),
                  pl.BlockSpec((1, rows, LANES), lambda bi, p, i: (bi, qoff + i, 0)),
                  pl.BlockSpec((1, tkeys, LANES), lambda bi, p, i: (bi, 0, p), pipeline_mode=nbuf),
                  pl.BlockSpec((1, tkeys, LANES), lambda bi, p, i: (bi, 0, 0), pipeline_mode=nbuf),
                  pl.BlockSpec((1, LANES, LANES), lambda bi, p, i: (p, 0, 0)),
                  pl.BlockSpec((1, LANES, tkeys), lambda bi, p, i: (bi, p, 0), pipeline_mode=nbuf)],
        out_specs=pl.BlockSpec((1, rows, LANES), lambda bi, p, i: (bi, i, p)),
        out_shape=jax.ShapeDtypeStruct((b, t, FOX_DIM), BF16),
        scratch_shapes=[pltpu.VMEM((tkeys, 2 * LANES), BF16),
                        pltpu.VMEM((tiles * nchain, 2 * LANES, tsub), BF16),
                        pltpu.VMEM((tiles * nchain, HEAD_DIM, tsub), F32),
                        pltpu.VMEM((nchain, tk, tsub), F32),
                        pltpu.VMEM((nchain, tk, tsub), F32)],
        compiler_params=_params(("arbitrary", "arbitrary", "arbitrary")),
        name="fox_attention",
    )(q, qx, k, kx, _pair_selectors(), vt)


def _ssd_kernel(xbc_ref, z_ref, dt_ref, cprev_ref, sprev_ref, wconv_ref, bconv_ref, alog_ref,
                dskip_ref, gssd_ref, rexp_ref,
                y_ref, hfin_ref, ext_ref, state_ref, *, L):
    c = pl.program_id(1)

    @pl.when(c == 0)
    def _():
        state_ref[...] = sprev_ref[0]
        ext_ref[...] = cprev_ref[0]

    x_tile = xbc_ref[0]
    ext = jnp.concatenate([ext_ref[...], x_tile], axis=0)
    w = wconv_ref[...]
    pre = bconv_ref[...] + x_tile * w[CONV_W - 1:CONV_W, :]
    for back in range(1, CONV_W):
        shifted = pltpu.roll(ext, back, axis=0)[SUBLANES:, :]
        pre = pre + shifted * w[CONV_W - 1 - back:CONV_W - back, :]
    ext_ref[...] = x_tile[L - SUBLANES:L, :]
    u = pre * _sigmoid(pre)
    xs = u[:, 0:SSD_DIM]
    bmat = u[:, SSD_DIM:SSD_DIM + SSD_GROUPS * SSD_STATE].astype(BF16)
    cmat = u[:, SSD_DIM + SSD_GROUPS * SSD_STATE:].astype(BF16)

    dt = dt_ref[0]
    a = dt * (-jnp.exp(alog_ref[...]))
    r = lax.broadcasted_iota(jnp.int32, (L, L), 0)
    cc = lax.broadcasted_iota(jnp.int32, (L, L), 1)
    causal = r >= cc
    ltri = jnp.where(causal, 1.0, 0.0).astype(BF16)
    cum = _sel_dot(ltri, a)
    e_r = lax.broadcasted_iota(jnp.int32, (N_HEADS, N_HEADS), 0)
    e_c = lax.broadcasted_iota(jnp.int32, (N_HEADS, N_HEADS), 1)
    eye = jnp.where(e_r == e_c, 1.0, 0.0).astype(BF16)
    cum_t = _sel_dot_nt(eye, cum)
    last = cum[L - 1:L, :]

    dt_t = _sel_dot_nt(eye, dt)

    rexp = rexp_ref[...]
    wendx = _dot_sel2(dt * jnp.exp(last - cum), rexp)
    ecumx = _dot_sel2(jnp.exp(cum), rexp)
    xs_b = xs.astype(BF16)
    xw_b = (xs * wendx).astype(BF16)
    elast_b = jnp.broadcast_to(jnp.exp(cum_t[:, L - 1:L]), (N_HEADS, SSD_STATE))
    lane = lax.broadcasted_iota(jnp.int32, (L, LANES), 1)

    heads_per_group = N_HEADS // SSD_GROUPS
    ys = []
    for g in range(SSD_GROUPS):
        gs = slice(GROUP_DIM * g, GROUP_DIM * (g + 1))
        cg = cmat[:, SSD_STATE * g:SSD_STATE * (g + 1)]
        bg = bmat[:, SSD_STATE * g:SSD_STATE * (g + 1)]
        cb = _dot_nt(cg, bg)
        st = state_ref[gs, :]
        yoff = _dot_nt(cg, st.astype(BF16))
        for pr in range(heads_per_group // 2):
            ps = slice(GROUP_DIM * g + LANES * pr, GROUP_DIM * g + LANES * (pr + 1))
            slab = xs_b[:, ps]
            res = []
            for hh in range(2):
                hd = heads_per_group * g + 2 * pr + hh
                seg = cum[:, hd:hd + 1] - cum_t[hd:hd + 1, :]
                dec = jnp.exp(jnp.where(causal, seg, -jnp.inf))
                res.append(_dot((cb * dec * dt_t[hd:hd + 1, :]).astype(BF16), slab))
            ydiag = jnp.where(lane < HEAD_DIM, res[0], res[1])
            ys.append(ydiag + yoff[:, LANES * pr:LANES * (pr + 1)] * ecumx[:, ps])
        snew = _dot_tn(xw_b[:, gs], bg)
        for hh in range(heads_per_group):
            hd = heads_per_group * g + hh
            rows = slice(HEAD_DIM * hd, HEAD_DIM * (hd + 1))
            state_ref[rows, :] = (elast_b[hd:hd + 1, :] * st[HEAD_DIM * hh:HEAD_DIM * (hh + 1), :]
                                  + snew[HEAD_DIM * hh:HEAD_DIM * (hh + 1), :])

    y = jnp.concatenate(ys, axis=-1) + dskip_ref[...] * xs
    zt = z_ref[0]
    ug = y * (zt * _sigmoid(zt))
    outs = []
    for g in range(SSD_GROUPS):
        ugg = ug[:, GROUP_DIM * g:GROUP_DIM * (g + 1)]
        outs.append(ugg * lax.rsqrt(jnp.mean(ugg * ugg, axis=-1, keepdims=True) + EPS))
    y_ref[0] = (jnp.concatenate(outs, axis=-1) * gssd_ref[...]).astype(BF16)

    @pl.when(c == pl.num_programs(1) - 1)
    def _():
        hfin_ref[0] = state_ref[...]


def _ssd(xbc, z, dt, conv_prev8, ssm_prev, w_conv, b_conv, a_log, d_skip, g_ssd, *, L):
    b, t, _ = xbc.shape
    assert t % L == 0 and L % SUBLANES == 0
    head_of_lane = jnp.arange(SSD_DIM) // HEAD_DIM
    rexp = (jnp.arange(N_HEADS)[:, None] == head_of_lane[None, :]).astype(BF16)
    tile = lambda w: pl.BlockSpec((1, L, w), lambda bi, ci: (bi, ci, 0))
    per_b = lambda s: pl.BlockSpec((1,) + s, lambda bi, ci: (bi, 0, 0))
    return pl.pallas_call(
        functools.partial(_ssd_kernel, L=L),
        grid=(b, t // L),
        in_specs=[tile(CONV_DIM), tile(SSD_DIM), tile(N_HEADS),
                  per_b((SUBLANES, CONV_DIM)), per_b((SSD_DIM, SSD_STATE)),
                  _resident((CONV_W, CONV_DIM)), _resident((1, CONV_DIM)), _resident((1, N_HEADS)),
                  _resident((1, SSD_DIM)), _resident((1, SSD_DIM)),
                  _resident((N_HEADS, SSD_DIM))],
        out_specs=(tile(SSD_DIM), per_b((SSD_DIM, SSD_STATE))),
        out_shape=(jax.ShapeDtypeStruct((b, t, SSD_DIM), BF16),
                   jax.ShapeDtypeStruct((b, SSD_DIM, SSD_STATE), F32)),
        scratch_shapes=[pltpu.VMEM((SUBLANES, CONV_DIM), F32),
                        pltpu.VMEM((SSD_DIM, SSD_STATE), F32)],
        compiler_params=_params(("arbitrary", "arbitrary")),
        name="ssd_scan",
    )(xbc, z, dt, conv_prev8, ssm_prev, w_conv, b_conv.reshape(1, CONV_DIM), a_log.reshape(1, N_HEADS),
      jnp.repeat(d_skip, HEAD_DIM).reshape(1, SSD_DIM), g_ssd.reshape(1, SSD_DIM), rexp)


def _ffn_kernel(x_ref, attn_ref, y_ref, wo_ref, g_ref, wg_ref, wu_ref, wd_ref, o_ref, hf_ref):
    j = pl.program_id(1)

    @pl.when(j == 0)
    def _():
        mix = _dot(attn_ref[...], wo_ref[0:FOX_DIM, :]) + _dot(y_ref[...], wo_ref[FOX_DIM:, :])
        x = x_ref[...] + mix
        ms = jnp.mean(x * x, axis=-1, keepdims=True)
        hf_ref[...] = (x * lax.rsqrt(ms + EPS) * g_ref[...]).astype(BF16)
        o_ref[...] = x

    hf = hf_ref[...]
    gate = _dot(hf, wg_ref[...])
    act = (gate * _sigmoid(gate) * _dot(hf, wu_ref[...])).astype(BF16)
    o_ref[...] += _dot(act, wd_ref[...])


def _outproj_ffn(x2d, attn, y, w_out, g_ffn, w_gate, w_up, w_down):
    n, d = x2d.shape
    dff = w_gate.shape[1]
    tm, tf = min(FFN_TOKENS, n), FFN_FF_TILE
    assert n % tm == 0 and dff % tf == 0
    row = lambda w: pl.BlockSpec((tm, w), lambda i, j: (i, 0))
    return pl.pallas_call(
        _ffn_kernel,
        grid=(n // tm, dff // tf),
        in_specs=[row(d), row(FOX_DIM), row(SSD_DIM), _resident((FOX_DIM + SSD_DIM, d)),
                  _resident((1, d)),
                  pl.BlockSpec((d, tf), lambda i, j: (0, j)),
                  pl.BlockSpec((d, tf), lambda i, j: (0, j)),
                  pl.BlockSpec((tf, d), lambda i, j: (j, 0))],
        out_specs=pl.BlockSpec((tm, d), lambda i, j: (i, 0)),
        out_shape=jax.ShapeDtypeStruct((n, d), F32),
        scratch_shapes=[pltpu.VMEM((tm, d), BF16)],
        compiler_params=_params(("arbitrary", "arbitrary")),
        name="ffn",
    )(x2d, attn, y, w_out, g_ffn.reshape(1, d), w_gate, w_up, w_down)


def _pack_w_in(w_in):
    d = w_in.shape[0]
    o_f = 3 * FOX_DIM
    o_z = o_f + N_HEADS
    o_xbc = o_z + SSD_DIM
    o_dt = o_xbc + CONV_DIM
    pad = jnp.zeros((d, LANES - N_HEADS), w_in.dtype)
    return jnp.concatenate([w_in[:, :o_f], w_in[:, o_z:o_xbc], w_in[:, o_xbc:o_dt],
                            w_in[:, o_f:o_z], pad, w_in[:, o_dt:], pad], axis=1).astype(BF16)


def _tile_plan(t, past):
    if past == 0:
        tile = min(4 * MXU_DIM, t)
        return dict(attn_tq=tile, attn_tk=tile, fbias_blk=min(MXU_DIM, t), ssd_chunk=min(LANES, t))
    return dict(attn_tq=t, attn_tk=LANES, fbias_blk=LANES, ssd_chunk=t)


def _layer(x, conv_prev, ssm_prev, k_prev, v_prev, logf_prev, wts):
    (g_mix, w_all, f_bias, g_q, g_k, w_conv, b_conv, dt_bias, a_log, d_skip, g_ssd,
     w_out, g_ffn, w_gate, w_up, w_down) = wts
    b, t, d = x.shape
    n = b * t
    x2d = x.reshape(n, d)
    qb, k, kb, v, vb, z, xbc, logf, dt = _inproj(x2d, t, g_mix, w_all, g_q, g_k, f_bias, dt_bias)

    past = 0 if k_prev is None else k_prev.shape[1]
    plan = _tile_plan(t, past)
    kb3, logf3 = kb.reshape(b, t, FOX_DIM), logf.reshape(b, t, N_HEADS)
    vt3 = vb if vb.ndim == 3 else jnp.transpose(vb.reshape(b, t, FOX_DIM), (0, 2, 1))
    logf_all = logf3
    if past:
        kb3 = jnp.concatenate([k_prev.reshape(b, past, FOX_DIM).astype(BF16), kb3], axis=1)
        vt_prev = jnp.transpose(v_prev.reshape(b, past, FOX_DIM).astype(BF16), (0, 2, 1))
        vt3 = jnp.concatenate([vt_prev, vt3], axis=2)
        logf_all = jnp.concatenate([logf_prev.astype(F32), logf3], axis=1)
    tkeys = past + t
    tk = plan["attn_tk"]
    if tkeys % tk:
        tk = -(-tkeys // LANES) * LANES
        grow = ((0, 0), (0, tk - tkeys), (0, 0))
        kb3, logf_all = jnp.pad(kb3, grow), jnp.pad(logf_all, grow)
        vt3 = jnp.pad(vt3, ((0, 0), (0, 0), (0, tk - tkeys)))
    kx, qx = _fbias(logf_all, min(plan["fbias_blk"], tk))
    attn = _attention(qb.reshape(b, t, FOX_DIM), qx, kb3, kx, vt3, past=past, tq=plan["attn_tq"], tk=tk)

    conv_prev8 = jnp.pad(conv_prev.astype(F32), ((0, 0), (SUBLANES - (CONV_W - 1), 0), (0, 0)))
    y, h_last = _ssd(xbc.reshape(b, t, CONV_DIM), z.reshape(b, t, SSD_DIM), dt.reshape(b, t, N_HEADS),
                     conv_prev8, ssm_prev.astype(F32).reshape(b, SSD_DIM, SSD_STATE),
                     w_conv, b_conv, a_log, d_skip, g_ssd, L=plan["ssd_chunk"])

    out = _outproj_ffn(x2d, attn.reshape(n, FOX_DIM), y.reshape(n, SSD_DIM), w_out, g_ffn, w_gate, w_up, w_down)

    xbc_all = jnp.concatenate([conv_prev.astype(F32), xbc.reshape(b, t, CONV_DIM)], axis=1)
    return (out.reshape(b, t, d), xbc_all[:, -(CONV_W - 1):],
            h_last.reshape(b, N_HEADS, HEAD_DIM, SSD_STATE),
            k.reshape(b, t, N_HEADS, HEAD_DIM), v.reshape(b, t, N_HEADS, HEAD_DIM), logf3)


def kernel(x_prompt, x_sample, cache_conv, state_ssm, cache_fox_k, cache_fox_v, cache_fox_logf, g_mix, w_in, f_bias, g_q, g_k, w_conv, b_conv, dt_bias, a_log, d_skip, g_ssd, w_out, g_ffn, w_gate, w_up, w_down):
    depth = g_mix.shape[0]
    yp, ys = x_prompt, x_sample
    bp = x_prompt.shape[0]
    outs_p = [[] for _ in range(5)]
    outs_s = [[] for _ in range(5)]
    for i in range(depth):
        wts = (g_mix[i], _pack_w_in(w_in[i]), f_bias[i], g_q[i], g_k[i], w_conv[i], b_conv[i], dt_bias[i],
               a_log[i], d_skip[i], g_ssd[i], w_out[i].astype(BF16), g_ffn[i],
               w_gate[i].astype(BF16), w_up[i].astype(BF16), w_down[i].astype(BF16))
        yp, *rest = _layer(
            yp, jnp.zeros((bp, CONV_W - 1, CONV_DIM), F32),
            jnp.zeros((bp, N_HEADS, HEAD_DIM, SSD_STATE), F32), None, None, None, wts)
        for lst, val in zip(outs_p, rest):
            lst.append(val)
        ys, *rest = _layer(
            ys, cache_conv[i], state_ssm[i], cache_fox_k[i], cache_fox_v[i], cache_fox_logf[i], wts)
        for lst, val in zip(outs_s, rest):
            lst.append(val)
    return (yp, ys, *[jnp.stack(l) for l in outs_p], *[jnp.stack(l) for l in outs_s])
```
